```python
import jax, jax.numpy as jnp
from jax import lax
import numpy as np

D_MODEL = 2048
BATCH = 4
SEQ = 2048
DEPTH = 4

N_MIXERS = 2
N_LAYERS_A = (DEPTH + 1) // 2
N_LAYERS_B = DEPTH // 2
EPS = 1e-6

D_FF = 5632

GM_CHUNK = 128
GM_INNER = 2 * D_MODEL
GM_GROUPS = 16
GM_GROUP_DIM = GM_INNER // GM_GROUPS

SSM_INNER = 2 * D_MODEL
SSM_HEAD_DIM = 64
SSM_HEADS = SSM_INNER // SSM_HEAD_DIM
SSM_GROUPS = 8
SSM_HEADS_PER_GROUP = SSM_HEADS // SSM_GROUPS
SSM_STATE = 128
SSM_CONV = 4
SSM_CHUNK = 128
SSM_BC = SSM_GROUPS * SSM_STATE
SSM_CONV_DIM = SSM_INNER + 2 * SSM_BC
SSM_PROJ = SSM_INNER + SSM_CONV_DIM + SSM_HEADS

kernel_name = 'hybrid_gmlp_ssd_macaron_trunk'


def rmsnorm(x, g):
    xf = x.astype(jnp.float32)
    y = xf * lax.rsqrt(jnp.mean(xf * xf, axis=-1, keepdims=True) + EPS)
    return (y * g.astype(jnp.float32)).astype(x.dtype)


def swiglu(h, w_in, w_out):
    gate, up = jnp.split(h @ w_in, 2, axis=-1)
    return (jax.nn.silu(gate) * up) @ w_out


def gmlp_mixer(h, w_in, v_norm, w_s, b_s, w_out):
    bsz, seq, _ = h.shape
    n_chunks = seq // GM_CHUNK
    z = jax.nn.gelu(h @ w_in, approximate=False)
    u, v = jnp.split(z, 2, axis=-1)
    v = rmsnorm(v, v_norm)
    v = v.reshape(bsz, n_chunks, GM_CHUNK, GM_GROUPS, GM_GROUP_DIM)
    causal = jnp.tril(jnp.ones((GM_CHUNK, GM_CHUNK), dtype=bool))
    w_causal = jnp.where(causal[None], w_s, jnp.zeros_like(w_s))
    mixed = jnp.einsum('gts,bnsgc->bntgc', w_causal, v) + b_s.T[:, :, None]
    gated = u * mixed.reshape(bsz, seq, GM_INNER)
    return gated @ w_out


def ssd_mixer(h, w_in, conv_w, conv_b, dt_bias, a_log, d_skip, norm_g, w_out):
    bsz, seq, _ = h.shape
    n_chunks = seq // SSM_CHUNK
    f32 = jnp.float32
    proj = h @ w_in
    z, xbc, dt = jnp.split(proj, [SSM_INNER, SSM_INNER + SSM_CONV_DIM], axis=-1)
    xbc = lax.conv_general_dilated(
        xbc, conv_w.T[:, None, :], window_strides=(1,), padding=[(SSM_CONV - 1, 0)],
        dimension_numbers=('NWC', 'WIO', 'NWC'), feature_group_count=SSM_CONV_DIM) + conv_b
    xbc = jax.nn.silu(xbc)
    xs, bm, cm = jnp.split(xbc, [SSM_INNER, SSM_INNER + SSM_BC], axis=-1)

    dt = jax.nn.softplus(dt.astype(f32) + dt_bias.astype(f32))
    a = -jnp.exp(a_log.astype(f32)).reshape(SSM_GROUPS, SSM_HEADS_PER_GROUP)
    x_h = xs.astype(f32).reshape(bsz, n_chunks, SSM_CHUNK, SSM_GROUPS, SSM_HEADS_PER_GROUP, SSM_HEAD_DIM)
    b_g = bm.astype(f32).reshape(bsz, n_chunks, SSM_CHUNK, SSM_GROUPS, SSM_STATE)
    c_g = cm.astype(f32).reshape(bsz, n_chunks, SSM_CHUNK, SSM_GROUPS, SSM_STATE)
    dt_c = dt.reshape(bsz, n_chunks, SSM_CHUNK, SSM_GROUPS, SSM_HEADS_PER_GROUP)
    xdt = x_h * dt_c[..., None]
    a_cum = jnp.cumsum(dt_c * a, axis=2).transpose(0, 1, 3, 4, 2)

    causal = jnp.tril(jnp.ones((SSM_CHUNK, SSM_CHUNK), dtype=bool))
    seg = a_cum[..., :, None] - a_cum[..., None, :]
    decay = jnp.exp(jnp.where(causal, seg, -jnp.inf))
    cb = jnp.einsum('bclgn,bcsgn->bcgls', c_g, b_g)
    scores = cb[:, :, :, None] * decay
    y_diag = jnp.einsum('bcgjls,bcsgjp->bclgjp', scores, xdt)

    decay_to_end = jnp.exp(a_cum[..., -1:] - a_cum)
    states = jnp.einsum('bclgn,bcgjl,bclgjp->bcgjpn', b_g, decay_to_end, xdt)
    chunk_decay = jnp.exp(a_cum[..., -1])

    def step(carry, inp):
        st, dec = inp
        return carry * dec[..., None, None] + st, carry

    init = jnp.zeros((bsz, SSM_GROUPS, SSM_HEADS_PER_GROUP, SSM_HEAD_DIM, SSM_STATE), f32)
    _, prev_states = lax.scan(step, init, (jnp.moveaxis(states, 1, 0), jnp.moveaxis(chunk_decay, 1, 0)))
    prev_states = jnp.moveaxis(prev_states, 0, 1)
    y_off = jnp.einsum('bclgn,bcgjpn,bcgjl->bclgjp', c_g, prev_states, jnp.exp(a_cum))

    y = y_diag + y_off + x_h * d_skip.astype(f32).reshape(SSM_GROUPS, SSM_HEADS_PER_GROUP)[:, :, None]
    y = y.reshape(bsz, seq, SSM_INNER) * jax.nn.silu(z.astype(f32))
    yg = y.reshape(bsz, seq, SSM_GROUPS, SSM_INNER // SSM_GROUPS)
    yg = yg * lax.rsqrt(jnp.mean(yg * yg, axis=-1, keepdims=True) + EPS)
    y = (yg.reshape(bsz, seq, SSM_INNER) * norm_g.astype(f32)).astype(h.dtype)
    return y @ w_out


def setup_inputs(seed: int = 0) -> dict:
    key = jax.random.key(seed)
    ks = jax.random.split(key, 32)

    def dense(k, shape, fan_in):
        return jax.random.normal(k, shape, jnp.float32) * (fan_in ** -0.5)

    def gain(k, shape):
        return 1.0 + 0.05 * jax.random.normal(k, shape, jnp.float32)

    x = jax.random.normal(ks[0], (BATCH, SEQ, D_MODEL), jnp.float32)
    dt0 = jnp.exp(jax.random.uniform(ks[20], (N_LAYERS_B, SSM_HEADS), jnp.float32)
                  * (np.log(0.1) - np.log(0.001)) + np.log(0.001))
    return {
        'x': x,
        'ln_ffn_pre': gain(ks[1], (DEPTH, D_MODEL)),
        'ffn_pre_w_in': dense(ks[2], (DEPTH, D_MODEL, 2 * D_FF), D_MODEL),
        'ffn_pre_w_out': dense(ks[3], (DEPTH, D_FF, D_MODEL), D_FF),
        'ln_mix': gain(ks[4], (DEPTH, D_MODEL)),
        'ln_ffn_post': gain(ks[5], (DEPTH, D_MODEL)),
        'ffn_post_w_in': dense(ks[6], (DEPTH, D_MODEL, 2 * D_FF), D_MODEL),
        'ffn_post_w_out': dense(ks[7], (DEPTH, D_FF, D_MODEL), D_FF),
        'gm_w_in': dense(ks[8], (N_LAYERS_A, D_MODEL, 2 * GM_INNER), D_MODEL),
        'gm_v_norm': gain(ks[9], (N_LAYERS_A, GM_INNER)),
        'gm_w_s': dense(ks[10], (N_LAYERS_A, GM_GROUPS, GM_CHUNK, GM_CHUNK), GM_CHUNK),
        'gm_b_s': 1.0 + 0.1 * jax.random.normal(ks[11], (N_LAYERS_A, GM_GROUPS, GM_CHUNK), jnp.float32),
        'gm_w_out': dense(ks[12], (N_LAYERS_A, GM_INNER, D_MODEL), GM_INNER),
        'ssm_w_in': dense(ks[13], (N_LAYERS_B, D_MODEL, SSM_PROJ), D_MODEL),
        'ssm_conv_w': dense(ks[14], (N_LAYERS_B, SSM_CONV_DIM, SSM_CONV), SSM_CONV),
        'ssm_conv_b': 0.01 * jax.random.normal(ks[15], (N_LAYERS_B, SSM_CONV_DIM), jnp.float32),
        'ssm_dt_bias': dt0 + jnp.log(-jnp.expm1(-dt0)),
        'ssm_a_log': jnp.log(jax.random.uniform(ks[16], (N_LAYERS_B, SSM_HEADS), jnp.float32, 1.0, 16.0)),
        'ssm_d': gain(ks[17], (N_LAYERS_B, SSM_HEADS)),
        'ssm_norm': gain(ks[18], (N_LAYERS_B, SSM_INNER)),
        'ssm_w_out': dense(ks[19], (N_LAYERS_B, SSM_INNER, D_MODEL), SSM_INNER),
        'ln_final': gain(ks[21], (D_MODEL,)),
    }


def reference(x, ln_ffn_pre, ffn_pre_w_in, ffn_pre_w_out, ln_mix, ln_ffn_post, ffn_post_w_in,
              ffn_post_w_out, gm_w_in, gm_v_norm, gm_w_s, gm_b_s, gm_w_out, ssm_w_in, ssm_conv_w,
              ssm_conv_b, ssm_dt_bias, ssm_a_log, ssm_d, ssm_norm, ssm_w_out, ln_final):
    for i in range(DEPTH):
        x = x + 0.5 * swiglu(rmsnorm(x, ln_ffn_pre[i]), ffn_pre_w_in[i], ffn_pre_w_out[i])
        h = rmsnorm(x, ln_mix[i])
        j = i // N_MIXERS
        if i % N_MIXERS == 0:
            m = gmlp_mixer(h, gm_w_in[j], gm_v_norm[j], gm_w_s[j], gm_b_s[j], gm_w_out[j])
        else:
            m = ssd_mixer(h, ssm_w_in[j], ssm_conv_w[j], ssm_conv_b[j], ssm_dt_bias[j],
                          ssm_a_log[j], ssm_d[j], ssm_norm[j], ssm_w_out[j])
        x = x + m
        x = x + 0.5 * swiglu(rmsnorm(x, ln_ffn_post[i]), ffn_post_w_in[i], ffn_post_w_out[i])
    return rmsnorm(x, ln_final)
```

```python
import functools
import math

import jax
import jax.numpy as jnp
from jax import lax
from jax.experimental import pallas as pl
from jax.experimental.pallas import tpu as pltpu

F32 = jnp.float32
BF16 = jnp.bfloat16
EPS = 1e-6

V7X_VMEM_BYTES = 64 * 1024 * 1024
V7X_LANES = 128

GM_CHUNK = 128
GM_GROUPS = 16
SSM_HEAD_DIM = 64
SSM_GROUPS = 8
SSM_HEADS_PER_GROUP = 8
SSM_STATE = 128
SSM_CONV = 4
SSM_CHUNK = 128
SSM_GROUP_WIDTH = SSM_HEADS_PER_GROUP * SSM_HEAD_DIM
HEADS_PER_LANE_TILE = V7X_LANES // SSM_HEAD_DIM
CONV_HALO = 8


def _nbytes(shape, dtype):
    return math.prod(shape) * jnp.dtype(dtype).itemsize


def _params(semantics, blocks, scratch=(), temps=0):
    need = 2 * sum(_nbytes(s, d) for s, d in blocks)
    need += sum(_nbytes(s, d) for s, d in scratch) + temps
    limit = min(V7X_VMEM_BYTES - (4 << 20), max(32 << 20, need + (8 << 20)))
    return pltpu.CompilerParams(dimension_semantics=semantics, vmem_limit_bytes=limit)


def _rmsnorm_rows(x, gain):
    ms = jnp.mean(x * x, axis=-1, keepdims=True)
    return x * lax.rsqrt(ms + EPS) * gain


def _ffn_kernel(x_ref, g_ref, wg_ref, wu_ref, wo_ref, o_ref, h_ref):
    j = pl.program_id(1)

    @pl.when(j == 0)
    def _():
        x = x_ref[...]
        h_ref[...] = _rmsnorm_rows(x, g_ref[...]).astype(BF16)
        o_ref[...] = x

    h = h_ref[...]
    gate = jnp.dot(h, wg_ref[...], preferred_element_type=F32)
    up = jnp.dot(h, wu_ref[...], preferred_element_type=F32)
    act = (gate * jax.nn.sigmoid(gate) * 0.5 * up).astype(BF16)
    o_ref[...] += jnp.dot(act, wo_ref[...], preferred_element_type=F32)


def _ffn(x, gain, w_in, w_out, *, tm, tf):
    n_tok, d = x.shape
    d_ff = w_out.shape[0]
    nf = d_ff // tf
    blocks = [((tm, d), F32), ((1, d), F32), ((d, tf), BF16), ((d, tf), BF16),
              ((tf, d), BF16), ((tm, d), F32)]
    scratch = [((tm, d), BF16)]
    return pl.pallas_call(
        _ffn_kernel,
        grid=(n_tok // tm, nf),
        in_specs=[
            pl.BlockSpec((tm, d), lambda i, j: (i, 0)),
            pl.BlockSpec((1, d), lambda i, j: (0, 0)),
            pl.BlockSpec((d, tf), lambda i, j: (0, j)),
            pl.BlockSpec((d, tf), lambda i, j: (0, j + nf)),
            pl.BlockSpec((tf, d), lambda i, j: (j, 0)),
        ],
        out_specs=pl.BlockSpec((tm, d), lambda i, j: (i, 0)),
        out_shape=jax.ShapeDtypeStruct((n_tok, d), F32),
        scratch_shapes=[pltpu.VMEM(s, t) for s, t in scratch],
        compiler_params=_params(("parallel", "arbitrary"), blocks, scratch,
                                temps=4 * tm * tf * 4),
        name="ffn",
    )(x, gain, w_in, w_in, w_out)


def _gm_in_kernel(x_ref, g_ref, w_ref, vg_ref, u_ref, v_ref, h_ref, vraw_ref, *, nu):
    j = pl.program_id(1)
    nv = vraw_ref.shape[0]
    tn = vraw_ref.shape[2]

    @pl.when(j == 0)
    def _():
        h_ref[...] = _rmsnorm_rows(x_ref[...], g_ref[...]).astype(BF16)

    z = jnp.dot(h_ref[...], w_ref[...], preferred_element_type=F32)
    z = 0.5 * z * (1.0 + lax.erf(z * math.sqrt(0.5)))

    @pl.when(j < nu)
    def _():
        u_ref[...] = z

    @pl.when(j >= nu)
    def _():
        vraw_ref[j - nu] = z

    @pl.when(j == nu + nv - 1)
    def _():
        ssq = jnp.zeros((z.shape[0], 1), F32)
        for k in range(nv):
            vk = vraw_ref[k]
            ssq += jnp.sum(vk * vk, axis=-1, keepdims=True)
        scale = lax.rsqrt(ssq / (nv * tn) + EPS)
        for k in range(nv):
            sl = slice(k * tn, (k + 1) * tn)
            v_ref[:, sl] = (vraw_ref[k] * scale * vg_ref[:, sl]).astype(BF16)


def _gm_in(x, gain, w_in, v_gain, *, tm, tn):
    n_tok, d = x.shape
    inner = w_in.shape[1] // 2
    nu = inner // tn
    blocks = [((tm, d), F32), ((1, d), F32), ((d, tn), BF16), ((1, inner), F32),
              ((tm, tn), F32), ((tm, inner), BF16)]
    scratch = [((tm, d), BF16), ((nu, tm, tn), F32)]
    return pl.pallas_call(
        functools.partial(_gm_in_kernel, nu=nu),
        grid=(n_tok // tm, 2 * nu),
        in_specs=[
            pl.BlockSpec((tm, d), lambda i, j: (i, 0)),
            pl.BlockSpec((1, d), lambda i, j: (0, 0)),
            pl.BlockSpec((d, tn), lambda i, j: (0, j)),
            pl.BlockSpec((1, inner), lambda i, j: (0, 0)),
        ],
        out_specs=[
            pl.BlockSpec((tm, tn), lambda i, j: (i, jnp.minimum(j, nu - 1))),
            pl.BlockSpec((tm, inner), lambda i, j: (i, 0)),
        ],
        out_shape=[jax.ShapeDtypeStruct((n_tok, inner), F32),
                   jax.ShapeDtypeStruct((n_tok, inner), BF16)],
        scratch_shapes=[pltpu.VMEM(s, t) for s, t in scratch],
        compiler_params=_params(("parallel", "arbitrary"), blocks, scratch,
                                temps=4 * tm * tn * 4),
        name="gm_in",
    )(x, gain, w_in, v_gain)


def _gm_out_kernel(x_ref, u_ref, v_ref, ws_ref, bs_ref, wo_ref, o_ref, gated_ref):
    j = pl.program_id(1)

    @pl.when(j == 0)
    def _():
        o_ref[...] = x_ref[...]

    groups = ws_ref.shape[0]
    gw = v_ref.shape[1] // groups
    tm = v_ref.shape[0]
    t_idx = lax.broadcasted_iota(jnp.int32, (GM_CHUNK, GM_CHUNK), 0)
    s_idx = lax.broadcasted_iota(jnp.int32, (GM_CHUNK, GM_CHUNK), 1)
    causal = s_idx <= t_idx
    for g in range(groups):
        w = jnp.where(causal, ws_ref[g], 0.0).astype(BF16)
        b = bs_ref[g]
        cols = slice(g * gw, (g + 1) * gw)
        for c in range(tm // GM_CHUNK):
            rows = slice(c * GM_CHUNK, (c + 1) * GM_CHUNK)
            mixed = jnp.dot(w, v_ref[rows, cols], preferred_element_type=F32) + b
            gated_ref[rows, cols] = (u_ref[rows, cols] * mixed).astype(BF16)
    o_ref[...] += jnp.dot(gated_ref[...], wo_ref[...], preferred_element_type=F32)


def _gm_out(x, u, vn, w_s, b_s, w_out, *, tm, tc):
    n_tok, d = x.shape
    inner = u.shape[1]
    gw = inner // GM_GROUPS
    gpt = tc // gw
    blocks = [((tm, d), F32), ((tm, tc), F32), ((tm, tc), BF16),
              ((gpt, GM_CHUNK, GM_CHUNK), F32), ((gpt, GM_CHUNK, V7X_LANES), F32),
              ((tc, d), BF16), ((tm, d), F32)]
    scratch = [((tm, tc), BF16)]
    return pl.pallas_call(
        _gm_out_kernel,
        grid=(n_tok // tm, inner // tc),
        in_specs=[
            pl.BlockSpec((tm, d), lambda i, j: (i, 0)),
            pl.BlockSpec((tm, tc), lambda i, j: (i, j)),
            pl.BlockSpec((tm, tc), lambda i, j: (i, j)),
            pl.BlockSpec((gpt, GM_CHUNK, GM_CHUNK), lambda i, j: (j, 0, 0)),
            pl.BlockSpec((gpt, GM_CHUNK, 1), lambda i, j: (j, 0, 0)),
            pl.BlockSpec((tc, d), lambda i, j: (j, 0)),
        ],
        out_specs=pl.BlockSpec((tm, d), lambda i, j: (i, 0)),
        out_shape=jax.ShapeDtypeStruct((n_tok, d), F32),
        scratch_shapes=[pltpu.VMEM(s, t) for s, t in scratch],
        compiler_params=_params(("parallel", "arbitrary"), blocks, scratch,
                                temps=2 * tm * tc * 4),
        name="gm_out",
    )(x, u, vn, w_s, b_s[..., None], w_out)


def _ssd_in_kernel(x_ref, g_ref, w_ref, wdt_ref, zx_ref, dt_ref, h_ref):
    j = pl.program_id(1)

    @pl.when(j == 0)
    def _():
        h_ref[...] = _rmsnorm_rows(x_ref[...], g_ref[...]).astype(BF16)
        dt_ref[...] = jnp.dot(h_ref[...], wdt_ref[...], preferred_element_type=F32)

    zx_ref[...] = jnp.dot(h_ref[...], w_ref[...], preferred_element_type=F32)


def _ssd_in(x, gain, w_zx, w_dt, *, tm, tn):
    n_tok, d = x.shape
    n_out = w_zx.shape[1]
    blocks = [((tm, d), F32), ((1, d), F32), ((d, tn), BF16), ((d, V7X_LANES), BF16),
              ((tm, tn), F32), ((tm, V7X_LANES), F32)]
    scratch = [((tm, d), BF16)]
    return pl.pallas_call(
        _ssd_in_kernel,
        grid=(n_tok // tm, n_out // tn),
        in_specs=[
            pl.BlockSpec((tm, d), lambda i, j: (i, 0)),
            pl.BlockSpec((1, d), lambda i, j: (0, 0)),
            pl.BlockSpec((d, tn), lambda i, j: (0, j)),
            pl.BlockSpec((d, V7X_LANES), lambda i, j: (0, 0)),
        ],
        out_specs=[
            pl.BlockSpec((tm, tn), lambda i, j: (i, j)),
            pl.BlockSpec((tm, V7X_LANES), lambda i, j: (i, 0)),
        ],
        out_shape=[jax.ShapeDtypeStruct((n_tok, n_out), F32),
                   jax.ShapeDtypeStruct((n_tok, V7X_LANES), F32)],
        scratch_shapes=[pltpu.VMEM(s, t) for s, t in scratch],
        compiler_params=_params(("parallel", "arbitrary"), blocks, scratch,
                                temps=2 * tm * tn * 4),
        name="ssd_in",
    )(x, gain, w_zx, w_dt)


def _causal_conv_silu(pad_ref, raw, w_ref, b_ref, first_chunk):
    chunk = raw.shape[0]

    @pl.when(first_chunk)
    def _():
        pad_ref[0:CONV_HALO, :] = jnp.zeros((CONV_HALO, raw.shape[1]), F32)

    @pl.when(jnp.logical_not(first_chunk))
    def _():
        pad_ref[0:CONV_HALO, :] = pad_ref[chunk:chunk + CONV_HALO, :]

    pad_ref[CONV_HALO:CONV_HALO + chunk, :] = raw
    acc = b_ref[...] + w_ref[SSM_CONV - 1:SSM_CONV, :] * raw
    for k in range(SSM_CONV - 1):
        off = CONV_HALO - (SSM_CONV - 1) + k
        acc += w_ref[k:k + 1, :] * pad_ref[off:off + chunk, :]
    return acc * jax.nn.sigmoid(acc)


def _ssd_core_kernel(z_ref, xs_ref, b_ref, c_ref, dtt_ref,
                     wx_ref, wb_ref, wc_ref, bx_ref, bb_ref, bc_ref,
                     dtbt_ref, alt_ref, dsk_ref, ng_ref,
                     y_ref, state_ref, xpad_ref, bpad_ref, cpad_ref):
    first = pl.program_id(2) == 0
    L = SSM_CHUNK
    P = SSM_HEAD_DIM

    xs = _causal_conv_silu(xpad_ref, xs_ref[...], wx_ref, bx_ref, first)
    bm = _causal_conv_silu(bpad_ref, b_ref[...], wb_ref, bb_ref, first)
    cm = _causal_conv_silu(cpad_ref, c_ref[...], wc_ref, bc_ref, first)

    @pl.when(first)
    def _():
        state_ref[...] = jnp.zeros_like(state_ref)

    hpg = SSM_HEADS_PER_GROUP
    dtt = jax.nn.softplus(dtt_ref[0] + dtbt_ref[0])
    at = -jnp.exp(alt_ref[0])
    r_idx = lax.broadcasted_iota(jnp.int32, (L, L), 0)
    c_idx = lax.broadcasted_iota(jnp.int32, (L, L), 1)
    causal = c_idx <= r_idx
    upper = (r_idx <= c_idx).astype(F32)
    a_cum_t = jnp.dot(dtt * at, upper, preferred_element_type=F32,
                      precision=lax.Precision.HIGHEST)
    stack = jnp.concatenate(
        [a_cum_t, dtt, jnp.zeros((V7X_LANES - 2 * hpg, L), F32)], axis=0)
    stack_t = stack.T

    cb = lax.dot_general(cm.astype(BF16), bm.astype(BF16), (((1,), (1,)), ((), ())),
                         preferred_element_type=F32)
    xs_t = xs.T
    lane = lax.broadcasted_iota(jnp.int32, (L, V7X_LANES), 1)
    sub = lax.broadcasted_iota(jnp.int32, (V7X_LANES, L), 0)

    y_tiles = []
    for q in range(SSM_HEADS_PER_GROUP // HEADS_PER_LANE_TILE):
        cols = slice(q * V7X_LANES, (q + 1) * V7X_LANES)
        x_q = xs[:, cols]
        xt_q = xs_t[cols, :]
        st_q = state_ref[q]
        y_q = jnp.zeros((L, V7X_LANES), F32)
        st_new = jnp.zeros_like(st_q)
        for hh in range(HEADS_PER_LANE_TILE):
            j = q * HEADS_PER_LANE_TILE + hh
            mine = (lane >= hh * P) & (lane < (hh + 1) * P)
            mine_t = (sub >= hh * P) & (sub < (hh + 1) * P)
            col = stack_t[:, j:j + 1]
            dt_col = stack_t[:, hpg + j:hpg + j + 1]
            row = a_cum_t[j:j + 1, :]
            last = a_cum_t[j:j + 1, L - 1:L]
            decay = jnp.exp(jnp.where(causal, col - row, -jnp.inf))
            scores = (cb * decay).astype(BF16)
            xdt = jnp.where(mine, x_q * dt_col, 0.0).astype(BF16)
            xdt_t = jnp.where(mine_t, xt_q * dtt[j:j + 1, :], 0.0).astype(BF16)
            c_in = (cm * jnp.exp(col)).astype(BF16)
            b_out = (bm * jnp.exp(last - col)).astype(BF16)
            st_h = jnp.where(mine_t, st_q, 0.0)
            y_q += jnp.dot(scores, xdt, preferred_element_type=F32)
            y_q += lax.dot_general(c_in, st_h.astype(BF16), (((1,), (1,)), ((), ())),
                                   preferred_element_type=F32)
            y_q += jnp.where(mine, x_q * dsk_ref[0][:, j:j + 1], 0.0)
            st_new += st_h * jnp.exp(last)
            st_new += jnp.dot(xdt_t, b_out, preferred_element_type=F32)
        state_ref[q] = st_new
        y_tiles.append(y_q)

    y = jnp.concatenate(y_tiles, axis=1)
    zg = z_ref[...]
    y = y * (zg * jax.nn.sigmoid(zg))
    y = y * lax.rsqrt(jnp.mean(y * y, axis=-1, keepdims=True) + EPS)
    y_ref[...] = (y * ng_ref[...]).astype(BF16)


def _ssd_core(zx, dt_gt, conv_w_t, conv_b, dt_bias, a_log, d_skip, norm_g, *, batch, seq):
    n_tok = zx.shape[0]
    inner = norm_g.shape[1]
    nc = seq // SSM_CHUNK
    gw = SSM_GROUP_WIDTH
    n = SSM_STATE
    z_blk = 0
    x_blk = inner // gw
    b_blk = 2 * inner // n
    c_blk = b_blk + SSM_GROUPS
    wb_blk = inner // n
    wc_blk = wb_blk + SSM_GROUPS
    hpg = SSM_HEADS_PER_GROUP
    tok = lambda b, g, c: b * nc + c

    def vec(x):
        return x.reshape(SSM_GROUPS, 1, hpg)

    def vec_t(x):
        return x.reshape(SSM_GROUPS, hpg, 1)

    small = lambda shape: pl.BlockSpec(shape, lambda b, g, c: (g, 0, 0))
    blocks = [((SSM_CHUNK, gw), F32)] * 2 + [((SSM_CHUNK, n), F32)] * 2 + [((SSM_CHUNK, gw), BF16)]
    scratch = [((hpg // HEADS_PER_LANE_TILE, V7X_LANES, n), F32),
               ((SSM_CHUNK + CONV_HALO, gw), F32),
               ((SSM_CHUNK + CONV_HALO, n), F32),
               ((SSM_CHUNK + CONV_HALO, n), F32)]
    return pl.pallas_call(
        _ssd_core_kernel,
        grid=(batch, SSM_GROUPS, nc),
        in_specs=[
            pl.BlockSpec((SSM_CHUNK, gw), lambda b, g, c: (tok(b, g, c), z_blk + g)),
            pl.BlockSpec((SSM_CHUNK, gw), lambda b, g, c: (tok(b, g, c), x_blk + g)),
            pl.BlockSpec((SSM_CHUNK, n), lambda b, g, c: (tok(b, g, c), b_blk + g)),
            pl.BlockSpec((SSM_CHUNK, n), lambda b, g, c: (tok(b, g, c), c_blk + g)),
            pl.BlockSpec((1, hpg, SSM_CHUNK), lambda b, g, c: (g, 0, tok(b, g, c))),
            pl.BlockSpec((SSM_CONV, gw), lambda b, g, c: (0, g)),
            pl.BlockSpec((SSM_CONV, n), lambda b, g, c: (0, wb_blk + g)),
            pl.BlockSpec((SSM_CONV, n), lambda b, g, c: (0, wc_blk + g)),
            pl.BlockSpec((1, gw), lambda b, g, c: (0, g)),
            pl.BlockSpec((1, n), lambda b, g, c: (0, wb_blk + g)),
            pl.BlockSpec((1, n), lambda b, g, c: (0, wc_blk + g)),
            small((1, hpg, 1)), small((1, hpg, 1)), small((1, 1, hpg)),
            pl.BlockSpec((1, gw), lambda b, g, c: (0, g)),
        ],
        out_specs=pl.BlockSpec((SSM_CHUNK, gw), lambda b, g, c: (tok(b, g, c), g)),
        out_shape=jax.ShapeDtypeStruct((n_tok, inner), BF16),
        scratch_shapes=[pltpu.VMEM(s, t) for s, t in scratch],
        compiler_params=_params(("parallel", "parallel", "arbitrary"), blocks, scratch,
                                temps=8 << 20),
        name="ssd_core",
    )(zx, zx, zx, zx, dt_gt, conv_w_t, conv_w_t, conv_w_t, conv_b, conv_b, conv_b,
      vec_t(dt_bias), vec_t(a_log), vec(d_skip), norm_g)


def _proj_res_kernel(y_ref, w_ref, x_ref, o_ref):
    o_ref[...] = x_ref[...] + jnp.dot(y_ref[...], w_ref[...], preferred_element_type=F32)


def _proj_res(y, w, x, *, tm, tn):
    n_tok, k = y.shape
    d = w.shape[1]
    blocks = [((tm, k), BF16), ((k, tn), BF16), ((tm, tn), F32), ((tm, tn), F32)]
    return pl.pallas_call(
        _proj_res_kernel,
        grid=(d // tn, n_tok // tm),
        in_specs=[
            pl.BlockSpec((tm, k), lambda j, i: (i, 0)),
            pl.BlockSpec((k, tn), lambda j, i: (0, j)),
            pl.BlockSpec((tm, tn), lambda j, i: (i, j)),
        ],
        out_specs=pl.BlockSpec((tm, tn), lambda j, i: (i, j)),
        out_shape=jax.ShapeDtypeStruct((n_tok, d), F32),
        compiler_params=_params(("parallel", "parallel"), blocks, temps=tm * tn * 4),
        name="proj_res",
    )(y, w, x)


def _final_norm_kernel(x_ref, g_ref, o_ref):
    o_ref[...] = _rmsnorm_rows(x_ref[...], g_ref[...])


def _final_norm(x, gain, *, tm):
    n_tok, d = x.shape
    blocks = [((tm, d), F32), ((1, d), F32), ((tm, d), F32)]
    return pl.pallas_call(
        _final_norm_kernel,
        grid=(n_tok // tm,),
        in_specs=[pl.BlockSpec((tm, d), lambda i: (i, 0)),
                  pl.BlockSpec((1, d), lambda i: (0, 0))],
        out_specs=pl.BlockSpec((tm, d), lambda i: (i, 0)),
        out_shape=jax.ShapeDtypeStruct((n_tok, d), F32),
        compiler_params=_params(("parallel",), blocks, temps=2 * tm * d * 4),
        name="final_norm",
    )(x, gain)


def kernel(x, ln_ffn_pre, ffn_pre_w_in, ffn_pre_w_out, ln_mix, ln_ffn_post, ffn_post_w_in,
           ffn_post_w_out, gm_w_in, gm_v_norm, gm_w_s, gm_b_s, gm_w_out, ssm_w_in, ssm_conv_w,
           ssm_conv_b, ssm_dt_bias, ssm_a_log, ssm_d, ssm_norm, ssm_w_out, ln_final):
    batch, seq, d = x.shape
    depth = ln_mix.shape[0]
    n_tok = batch * seq
    inner = ssm_norm.shape[1]
    heads = ssm_dt_bias.shape[1]
    zx_cols = ssm_conv_w.shape[1] + inner
    assert seq % SSM_CHUNK == 0 and seq % GM_CHUNK == 0
    assert heads == SSM_GROUPS * SSM_HEADS_PER_GROUP and inner == heads * SSM_HEAD_DIM

    row = lambda v: v.reshape(1, -1)
    xf = x.reshape(n_tok, d)
    for i in range(depth):
        xf = _ffn(xf, row(ln_ffn_pre[i]), ffn_pre_w_in[i].astype(BF16),
                  ffn_pre_w_out[i].astype(BF16), tm=512, tf=512)
        m = i // 2
        if i % 2 == 0:
            u, vn = _gm_in(xf, row(ln_mix[i]), gm_w_in[m].astype(BF16), row(gm_v_norm[m]),
                           tm=512, tn=512)
            xf = _gm_out(xf, u, vn, gm_w_s[m], gm_b_s[m], gm_w_out[m].astype(BF16),
                         tm=512, tc=512)
        else:
            w_in = ssm_w_in[m]
            w_zx = w_in[:, :zx_cols].astype(BF16)
            w_dt = jnp.pad(w_in[:, zx_cols:], ((0, 0), (0, V7X_LANES - heads))).astype(BF16)
            zx, dt = _ssd_in(xf, row(ln_mix[i]), w_zx, w_dt, tm=512, tn=512)
            dt_gt = dt[:, :heads].reshape(n_tok, SSM_GROUPS, SSM_HEADS_PER_GROUP)
            dt_gt = dt_gt.transpose(1, 2, 0)
            y = _ssd_core(zx, dt_gt, ssm_conv_w[m].T, row(ssm_conv_b[m]),
                          ssm_dt_bias[m], ssm_a_log[m], ssm_d[m], row(ssm_norm[m]),
                          batch=batch, seq=seq)
            xf = _proj_res(y, ssm_w_out[m].astype(BF16), xf, tm=512, tn=1024)
        xf = _ffn(xf, row(ln_ffn_post[i]), ffn_post_w_in[i].astype(BF16),
                  ffn_post_w_out[i].astype(BF16), tm=512, tf=512)
    out = _final_norm(xf, row(ln_final), tm=512)
    return out.reshape(batch, seq, d)
```

```python
import functools
import math

import jax
import jax.numpy as jnp
from jax import lax
from jax.experimental import pallas as pl
from jax.experimental.pallas import tpu as pltpu

F32 = jnp.float32
BF16 = jnp.bfloat16
EPS = 1e-6

V7X_VMEM_BYTES = 64 * 1024 * 1024
V7X_LANES = 128

GM_CHUNK = 128
GM_GROUPS = 16
SSM_HEAD_DIM = 64
SSM_GROUPS = 8
SSM_HEADS_PER_GROUP = 8
SSM_STATE = 128
SSM_CONV = 4
SSM_CHUNK = 128
SSM_GROUP_WIDTH = SSM_HEADS_PER_GROUP * SSM_HEAD_DIM
HEADS_PER_LANE_TILE = V7X_LANES // SSM_HEAD_DIM
CONV_HALO = 8


def _nbytes(shape, dtype):
    return math.prod(shape) * jnp.dtype(dtype).itemsize


def _params(semantics, blocks, scratch=(), temps=0):
    need = 2 * sum(_nbytes(s, d) for s, d in blocks)
    need += sum(_nbytes(s, d) for s, d in scratch) + temps
    limit = min(V7X_VMEM_BYTES - (4 << 20), max(32 << 20, need + (8 << 20)))
    return pltpu.CompilerParams(dimension_semantics=semantics, vmem_limit_bytes=limit)


def _rmsnorm_rows(x, gain):
    ms = jnp.mean(x * x, axis=-1, keepdims=True)
    return x * lax.rsqrt(ms + EPS) * gain


def _ffn_kernel(x_ref, g_ref, wg_ref, wu_ref, wo_ref, o_ref, h_ref):
    j = pl.program_id(1)

    @pl.when(j == 0)
    def _():
        x = x_ref[...]
        h_ref[...] = _rmsnorm_rows(x, g_ref[...]).astype(BF16)
        o_ref[...] = x

    h = h_ref[...]
    gate = jnp.dot(h, wg_ref[...].astype(BF16), preferred_element_type=F32)
    up = jnp.dot(h, wu_ref[...].astype(BF16), preferred_element_type=F32)
    act = (gate * jax.nn.sigmoid(gate) * 0.5 * up).astype(BF16)
    o_ref[...] += jnp.dot(act, wo_ref[...].astype(BF16), preferred_element_type=F32)


def _ffn(x, gain, w_in, w_out, *, tm, tf):
    n_tok, d = x.shape
    d_ff = w_out.shape[0]
    nf = d_ff // tf
    blocks = [((tm, d), F32), ((1, d), F32), ((d, tf), F32), ((d, tf), F32),
              ((tf, d), F32), ((tm, d), F32)]
    scratch = [((tm, d), BF16)]
    return pl.pallas_call(
        _ffn_kernel,
        grid=(n_tok // tm, nf),
        in_specs=[
            pl.BlockSpec((tm, d), lambda i, j: (i, 0)),
            pl.BlockSpec((1, d), lambda i, j: (0, 0)),
            pl.BlockSpec((d, tf), lambda i, j: (0, j)),
            pl.BlockSpec((d, tf), lambda i, j: (0, j + nf)),
            pl.BlockSpec((tf, d), lambda i, j: (j, 0)),
        ],
        out_specs=pl.BlockSpec((tm, d), lambda i, j: (i, 0)),
        out_shape=jax.ShapeDtypeStruct((n_tok, d), F32),
        scratch_shapes=[pltpu.VMEM(s, t) for s, t in scratch],
        compiler_params=_params(("parallel", "arbitrary"), blocks, scratch,
                                temps=4 * tm * tf * 4 + 3 * d * tf * 2),
        name="ffn",
    )(x, gain, w_in, w_in, w_out)


def _gm_in_kernel(x_ref, g_ref, w_ref, vg_ref, u_ref, v_ref, h_ref, vraw_ref, *, nu):
    j = pl.program_id(1)
    nv = vraw_ref.shape[0]
    tn = vraw_ref.shape[2]

    @pl.when(j == 0)
    def _():
        h_ref[...] = _rmsnorm_rows(x_ref[...], g_ref[...]).astype(BF16)

    z = jnp.dot(h_ref[...], w_ref[...], preferred_element_type=F32)
    z = 0.5 * z * (1.0 + lax.erf(z * math.sqrt(0.5)))

    @pl.when(j < nu)
    def _():
        u_ref[...] = z

    @pl.when(j >= nu)
    def _():
        vraw_ref[j - nu] = z

    @pl.when(j == nu + nv - 1)
    def _():
        ssq = jnp.zeros((z.shape[0], 1), F32)
        for k in range(nv):
            vk = vraw_ref[k]
            ssq += jnp.sum(vk * vk, axis=-1, keepdims=True)
        scale = lax.rsqrt(ssq / (nv * tn) + EPS)
        for k in range(nv):
            sl = slice(k * tn, (k + 1) * tn)
            v_ref[:, sl] = (vraw_ref[k] * scale * vg_ref[:, sl]).astype(BF16)


def _gm_in(x, gain, w_in, v_gain, *, tm, tn):
    n_tok, d = x.shape
    inner = w_in.shape[1] // 2
    nu = inner // tn
    blocks = [((tm, d), F32), ((1, d), F32), ((d, tn), BF16), ((1, inner), F32),
              ((tm, tn), F32), ((tm, inner), BF16)]
    scratch = [((tm, d), BF16), ((nu, tm, tn), F32)]
    return pl.pallas_call(
        functools.partial(_gm_in_kernel, nu=nu),
        grid=(n_tok // tm, 2 * nu),
        in_specs=[
            pl.BlockSpec((tm, d), lambda i, j: (i, 0)),
            pl.BlockSpec((1, d), lambda i, j: (0, 0)),
            pl.BlockSpec((d, tn), lambda i, j: (0, j)),
            pl.BlockSpec((1, inner), lambda i, j: (0, 0)),
        ],
        out_specs=[
            pl.BlockSpec((tm, tn), lambda i, j: (i, jnp.minimum(j, nu - 1))),
            pl.BlockSpec((tm, inner), lambda i, j: (i, 0)),
        ],
        out_shape=[jax.ShapeDtypeStruct((n_tok, inner), F32),
                   jax.ShapeDtypeStruct((n_tok, inner), BF16)],
        scratch_shapes=[pltpu.VMEM(s, t) for s, t in scratch],
        compiler_params=_params(("parallel", "arbitrary"), blocks, scratch,
                                temps=4 * tm * tn * 4),
        name="gm_in",
    )(x, gain, w_in, v_gain)


def _gm_out_kernel(x_ref, u_ref, v_ref, ws_ref, bs_ref, wo_ref, o_ref, gated_ref):
    j = pl.program_id(1)

    @pl.when(j == 0)
    def _():
        o_ref[...] = x_ref[...]

    groups = ws_ref.shape[0]
    gw = v_ref.shape[1] // groups
    tm = v_ref.shape[0]
    t_idx = lax.broadcasted_iota(jnp.int32, (GM_CHUNK, GM_CHUNK), 0)
    s_idx = lax.broadcasted_iota(jnp.int32, (GM_CHUNK, GM_CHUNK), 1)
    causal = s_idx <= t_idx
    for g in range(groups):
        w = jnp.where(causal, ws_ref[g], 0.0).astype(BF16)
        b = bs_ref[g]
        cols = slice(g * gw, (g + 1) * gw)
        for c in range(tm // GM_CHUNK):
            rows = slice(c * GM_CHUNK, (c + 1) * GM_CHUNK)
            mixed = jnp.dot(w, v_ref[rows, cols], preferred_element_type=F32) + b
            gated_ref[rows, cols] = (u_ref[rows, cols] * mixed).astype(BF16)
    o_ref[...] += jnp.dot(gated_ref[...], wo_ref[...], preferred_element_type=F32)


def _gm_out(x, u, vn, w_s, b_s, w_out, *, tm, tc):
    n_tok, d = x.shape
    inner = u.shape[1]
    gw = inner // GM_GROUPS
    gpt = tc // gw
    blocks = [((tm, d), F32), ((tm, tc), F32), ((tm, tc), BF16),
              ((gpt, GM_CHUNK, GM_CHUNK), F32), ((gpt, GM_CHUNK, V7X_LANES), F32),
              ((tc, d), BF16), ((tm, d), F32)]
    scratch = [((tm, tc), BF16)]
    return pl.pallas_call(
        _gm_out_kernel,
        grid=(n_tok // tm, inner // tc),
        in_specs=[
            pl.BlockSpec((tm, d), lambda i, j: (i, 0)),
            pl.BlockSpec((tm, tc), lambda i, j: (i, j)),
            pl.BlockSpec((tm, tc), lambda i, j: (i, j)),
            pl.BlockSpec((gpt, GM_CHUNK, GM_CHUNK), lambda i, j: (j, 0, 0)),
            pl.BlockSpec((gpt, GM_CHUNK, 1), lambda i, j: (j, 0, 0)),
            pl.BlockSpec((tc, d), lambda i, j: (j, 0)),
        ],
        out_specs=pl.BlockSpec((tm, d), lambda i, j: (i, 0)),
        out_shape=jax.ShapeDtypeStruct((n_tok, d), F32),
        scratch_shapes=[pltpu.VMEM(s, t) for s, t in scratch],
        compiler_params=_params(("parallel", "arbitrary"), blocks, scratch,
                                temps=2 * tm * tc * 4),
        name="gm_out",
    )(x, u, vn, w_s, b_s[..., None], w_out)


def _ssd_in_kernel(x_ref, g_ref, w_ref, wdt_ref, cw_ref, cb_ref, zx_ref, dt_ref,
                   h_ref, pad_ref, carry_ref, *, tiles_per_seq, nz):
    i = pl.program_id(0)
    j = pl.program_id(1)
    tm = zx_ref.shape[0]
    seq_start = (i % tiles_per_seq) == 0

    @pl.when(j == 0)
    def _():
        h_ref[...] = _rmsnorm_rows(x_ref[...], g_ref[...]).astype(BF16)
        dt_ref[...] = jnp.dot(h_ref[...], wdt_ref[...], preferred_element_type=F32)

    @pl.when(j < nz)
    def _():
        acc = jnp.dot(h_ref[...], w_ref[...], preferred_element_type=F32)
        zx_ref[...] = acc * jax.nn.sigmoid(acc)

    @pl.when(j >= nz)
    def _():
        t = j - nz

        @pl.when(seq_start)
        def _():
            pad_ref[0:CONV_HALO, :] = jnp.zeros((CONV_HALO, pad_ref.shape[1]), F32)

        @pl.when(jnp.logical_not(seq_start))
        def _():
            pad_ref[0:CONV_HALO, :] = carry_ref[t]

        acc = jnp.dot(h_ref[...], w_ref[...], preferred_element_type=F32)
        pad_ref[CONV_HALO:CONV_HALO + tm, :] = acc
        carry_ref[t] = acc[tm - CONV_HALO:tm, :]
        out = cb_ref[...] + cw_ref[SSM_CONV - 1:SSM_CONV, :] * acc
        for k in range(SSM_CONV - 1):
            off = CONV_HALO - (SSM_CONV - 1) + k
            out += cw_ref[k:k + 1, :] * pad_ref[off:off + tm, :]
        zx_ref[...] = out * jax.nn.sigmoid(out)


def _ssd_in(x, gain, w_zx, w_dt, conv_w_t, conv_b, *, tm, tn, seq):
    n_tok, d = x.shape
    n_out = w_zx.shape[1]
    nj = n_out // tn
    nz = nj - conv_w_t.shape[1] // tn
    blocks = [((tm, d), F32), ((1, d), F32), ((d, tn), BF16), ((d, V7X_LANES), BF16),
              ((8, tn), F32), ((8, tn), F32), ((tm, tn), F32), ((tm, V7X_LANES), F32)]
    scratch = [((tm, d), BF16), ((tm + CONV_HALO, tn), F32), ((nj - nz, CONV_HALO, tn), F32)]
    return pl.pallas_call(
        functools.partial(_ssd_in_kernel, tiles_per_seq=seq // tm, nz=nz),
        grid=(n_tok // tm, nj),
        in_specs=[
            pl.BlockSpec((tm, d), lambda i, j: (i, 0)),
            pl.BlockSpec((1, d), lambda i, j: (0, 0)),
            pl.BlockSpec((d, tn), lambda i, j: (0, j)),
            pl.BlockSpec((d, V7X_LANES), lambda i, j: (0, 0)),
            pl.BlockSpec((SSM_CONV, tn), lambda i, j: (0, jnp.maximum(j - nz, 0))),
            pl.BlockSpec((1, tn), lambda i, j: (0, jnp.maximum(j - nz, 0))),
        ],
        out_specs=[
            pl.BlockSpec((tm, tn), lambda i, j: (i, j)),
            pl.BlockSpec((tm, V7X_LANES), lambda i, j: (i, 0)),
        ],
        out_shape=[jax.ShapeDtypeStruct((n_tok, n_out), F32),
                   jax.ShapeDtypeStruct((n_tok, V7X_LANES), F32)],
        scratch_shapes=[pltpu.VMEM(s, t) for s, t in scratch],
        compiler_params=_params(("arbitrary", "arbitrary"), blocks, scratch,
                                temps=3 * tm * tn * 4),
        name="ssd_in",
    )(x, gain, w_zx, w_dt, conv_w_t, conv_b)


def _ssd_core_kernel(zs_ref, xs_ref, b_ref, c_ref, dtt_ref, dtb_ref, al_ref, dsk_ref, ng_ref,
                     y_ref, state_ref):
    L = SSM_CHUNK
    P = SSM_HEAD_DIM
    N = SSM_STATE
    hpg = SSM_HEADS_PER_GROUP
    gw = SSM_GROUP_WIDTH
    heads = dtt_ref.shape[0]

    @pl.when(pl.program_id(1) == 0)
    def _():
        state_ref[...] = jnp.zeros_like(state_ref)

    dtt = jax.nn.softplus(dtt_ref[...] + dtb_ref[...])
    at = -jnp.exp(al_ref[...])
    r_idx = lax.broadcasted_iota(jnp.int32, (L, L), 0)
    c_idx = lax.broadcasted_iota(jnp.int32, (L, L), 1)
    causal = c_idx <= r_idx
    upper = (r_idx <= c_idx).astype(F32)
    a_cum_t = jnp.dot(dtt * at, upper, preferred_element_type=F32,
                      precision=lax.Precision.HIGHEST)
    last_t = a_cum_t[:, L - 1:L]
    w_t = dtt * jnp.exp(last_t - a_cum_t)
    dk_b = jnp.exp(jnp.broadcast_to(last_t, (heads, N)))
    stack_t = jnp.concatenate(
        [a_cum_t, jnp.zeros((V7X_LANES - heads, L), F32)], axis=0).T
    low_half = lax.broadcasted_iota(jnp.int32, (L, V7X_LANES), 1) < P

    for g in range(SSM_GROUPS):
        gcols = slice(g * gw, (g + 1) * gw)
        bm = b_ref[:, g * N:(g + 1) * N].astype(BF16)
        cm = c_ref[:, g * N:(g + 1) * N].astype(BF16)
        xg = xs_ref[:, gcols]
        st = state_ref[gcols, :]
        hs = range(g * hpg, (g + 1) * hpg)

        cb = lax.dot_general(cm, bm, (((1,), (1,)), ((), ())),
                             preferred_element_type=F32)
        y_off = lax.dot_general(cm, st.astype(BF16), (((1,), (1,)), ((), ())),
                                preferred_element_type=F32)

        w_rows = jnp.concatenate(
            [jnp.broadcast_to(w_t[h:h + 1, :], (P, L)) for h in hs], axis=0)
        dk_rows = jnp.concatenate(
            [jnp.broadcast_to(dk_b[h:h + 1, :], (P, N)) for h in hs], axis=0)
        xw_t = (xg.T * w_rows).astype(BF16)
        state_ref[gcols, :] = st * dk_rows + jnp.dot(xw_t, bm, preferred_element_type=F32)

        y_tiles = []
        for q in range(hpg // HEADS_PER_LANE_TILE):
            qcols = slice(q * V7X_LANES, (q + 1) * V7X_LANES)
            x_q = xg[:, qcols]
            a_cols = []
            scores = []
            for hh in range(HEADS_PER_LANE_TILE):
                h = g * hpg + q * HEADS_PER_LANE_TILE + hh
                a_col = jnp.broadcast_to(stack_t[:, h:h + 1], (L, L))
                seg = a_col - a_cum_t[h:h + 1, :]
                decay = jnp.exp(jnp.where(causal, seg, -jnp.inf))
                scores.append((cb * decay * dtt[h:h + 1, :]).astype(BF16))
                a_cols.append(a_col)
            x_lo = jnp.where(low_half, x_q, 0.0).astype(BF16)
            x_hi = jnp.where(low_half, 0.0, x_q).astype(BF16)
            y_q = jnp.dot(jnp.concatenate(scores, axis=1),
                          jnp.concatenate([x_lo, x_hi], axis=0),
                          preferred_element_type=F32)
            e_q = jnp.exp(jnp.where(low_half, a_cols[0], a_cols[1]))
            y_tiles.append(y_q + y_off[:, qcols] * e_q)

        y = jnp.concatenate(y_tiles, axis=1) + xg * dsk_ref[:, gcols]
        y = y * zs_ref[:, gcols]
        y = y * lax.rsqrt(jnp.mean(y * y, axis=-1, keepdims=True) + EPS)
        y_ref[:, gcols] = (y * ng_ref[:, gcols]).astype(BF16)


def _ssd_core(zx, dtt, dt_bias, a_log, d_skip, norm_g, *, batch, seq):
    n_tok = zx.shape[0]
    inner = norm_g.shape[1]
    heads = dtt.shape[0]
    nc = seq // SSM_CHUNK
    bc_w = SSM_GROUPS * SSM_STATE
    tok = lambda b, c: b * nc + c
    col = lambda v: v.reshape(-1, 1)
    d_lanes = jnp.repeat(d_skip, SSM_HEAD_DIM).reshape(1, inner)
    blocks = [((SSM_CHUNK, inner), F32)] * 2 + [((SSM_CHUNK, bc_w), F32)] * 2 + \
             [((heads, SSM_CHUNK), F32), ((SSM_CHUNK, inner), BF16)]
    scratch = [((inner, SSM_STATE), F32)]
    return pl.pallas_call(
        _ssd_core_kernel,
        grid=(batch, nc),
        in_specs=[
            pl.BlockSpec((SSM_CHUNK, inner), lambda b, c: (tok(b, c), 0)),
            pl.BlockSpec((SSM_CHUNK, inner), lambda b, c: (tok(b, c), 1)),
            pl.BlockSpec((SSM_CHUNK, bc_w), lambda b, c: (tok(b, c), 2 * inner // bc_w)),
            pl.BlockSpec((SSM_CHUNK, bc_w), lambda b, c: (tok(b, c), 2 * inner // bc_w + 1)),
            pl.BlockSpec((heads, SSM_CHUNK), lambda b, c: (0, tok(b, c))),
            pl.BlockSpec((heads, 1), lambda b, c: (0, 0)),
            pl.BlockSpec((heads, 1), lambda b, c: (0, 0)),
            pl.BlockSpec((1, inner), lambda b, c: (0, 0)),
            pl.BlockSpec((1, inner), lambda b, c: (0, 0)),
        ],
        out_specs=pl.BlockSpec((SSM_CHUNK, inner), lambda b, c: (tok(b, c), 0)),
        out_shape=jax.ShapeDtypeStruct((n_tok, inner), BF16),
        scratch_shapes=[pltpu.VMEM(s, t) for s, t in scratch],
        compiler_params=_params(("parallel", "arbitrary"), blocks, scratch, temps=8 << 20),
        name="ssd_core",
    )(zx, zx, zx, zx, dtt, col(dt_bias), col(a_log), d_lanes, norm_g)


def _proj_res_kernel(y_ref, w_ref, x_ref, o_ref):
    o_ref[...] = x_ref[...] + jnp.dot(y_ref[...], w_ref[...], preferred_element_type=F32)


def _proj_res(y, w, x, *, tm, tn):
    n_tok, k = y.shape
    d = w.shape[1]
    blocks = [((tm, k), BF16), ((k, tn), BF16), ((tm, tn), F32), ((tm, tn), F32)]
    return pl.pallas_call(
        _proj_res_kernel,
        grid=(d // tn, n_tok // tm),
        in_specs=[
            pl.BlockSpec((tm, k), lambda j, i: (i, 0)),
            pl.BlockSpec((k, tn), lambda j, i: (0, j)),
            pl.BlockSpec((tm, tn), lambda j, i: (i, j)),
        ],
        out_specs=pl.BlockSpec((tm, tn), lambda j, i: (i, j)),
        out_shape=jax.ShapeDtypeStruct((n_tok, d), F32),
        compiler_params=_params(("parallel", "parallel"), blocks, temps=tm * tn * 4),
        name="proj_res",
    )(y, w, x)


def _final_norm_kernel(x_ref, g_ref, o_ref):
    o_ref[...] = _rmsnorm_rows(x_ref[...], g_ref[...])


def _final_norm(x, gain, *, tm):
    n_tok, d = x.shape
    blocks = [((tm, d), F32), ((1, d), F32), ((tm, d), F32)]
    return pl.pallas_call(
        _final_norm_kernel,
        grid=(n_tok // tm,),
        in_specs=[pl.BlockSpec((tm, d), lambda i: (i, 0)),
                  pl.BlockSpec((1, d), lambda i: (0, 0))],
        out_specs=pl.BlockSpec((tm, d), lambda i: (i, 0)),
        out_shape=jax.ShapeDtypeStruct((n_tok, d), F32),
        compiler_params=_params(("parallel",), blocks, temps=2 * tm * d * 4),
        name="final_norm",
    )(x, gain)


def kernel(x, ln_ffn_pre, ffn_pre_w_in, ffn_pre_w_out, ln_mix, ln_ffn_post, ffn_post_w_in,
           ffn_post_w_out, gm_w_in, gm_v_norm, gm_w_s, gm_b_s, gm_w_out, ssm_w_in, ssm_conv_w,
           ssm_conv_b, ssm_dt_bias, ssm_a_log, ssm_d, ssm_norm, ssm_w_out, ln_final):
    batch, seq, d = x.shape
    depth = ln_mix.shape[0]
    n_tok = batch * seq
    inner = ssm_norm.shape[1]
    heads = ssm_dt_bias.shape[1]
    zx_cols = ssm_conv_w.shape[1] + inner
    assert seq % SSM_CHUNK == 0 and seq % GM_CHUNK == 0
    assert heads == SSM_GROUPS * SSM_HEADS_PER_GROUP and inner == heads * SSM_HEAD_DIM

    row = lambda v: v.reshape(1, -1)
    xf = x.reshape(n_tok, d)
    for i in range(depth):
        xf = _ffn(xf, row(ln_ffn_pre[i]), ffn_pre_w_in[i], ffn_pre_w_out[i], tm=1024, tf=256)
        m = i // 2
        if i % 2 == 0:
            u, vn = _gm_in(xf, row(ln_mix[i]), gm_w_in[m].astype(BF16), row(gm_v_norm[m]),
                           tm=512, tn=1024)
            xf = _gm_out(xf, u, vn, gm_w_s[m], gm_b_s[m], gm_w_out[m].astype(BF16),
                         tm=512, tc=512)
        else:
            w_in = ssm_w_in[m]
            w_zx = w_in[:, :zx_cols].astype(BF16)
            w_dt = jnp.pad(w_in[:, zx_cols:], ((0, 0), (0, V7X_LANES - heads))).astype(BF16)
            zx, dt = _ssd_in(xf, row(ln_mix[i]), w_zx, w_dt, ssm_conv_w[m].T,
                             row(ssm_conv_b[m]), tm=512, tn=1024, seq=seq)
            dtt = dt[:, :heads].T
            y = _ssd_core(zx, dtt, ssm_dt_bias[m], ssm_a_log[m], ssm_d[m], row(ssm_norm[m]),
                          batch=batch, seq=seq)
            xf = _proj_res(y, ssm_w_out[m].astype(BF16), xf, tm=512, tn=1024)
        xf = _ffn(xf, row(ln_ffn_post[i]), ffn_post_w_in[i], ffn_post_w_out[i], tm=1024, tf=256)
    out = _final_norm(xf, row(ln_final), tm=512)
    return out.reshape(batch, seq, d)
```

```python
import functools
import math

import jax
import jax.numpy as jnp
from jax import lax
from jax.experimental import pallas as pl
from jax.experimental.pallas import tpu as pltpu

F32 = jnp.float32
BF16 = jnp.bfloat16
EPS = 1e-6

V7X_VMEM_BYTES = 64 * 1024 * 1024
V7X_LANES = 128

GM_CHUNK = 128
GM_GROUPS = 16
SSM_HEAD_DIM = 64
SSM_GROUPS = 8
SSM_HEADS_PER_GROUP = 8
SSM_STATE = 128
SSM_CONV = 4
SSM_CHUNK = 128
SSM_GROUP_WIDTH = SSM_HEADS_PER_GROUP * SSM_HEAD_DIM
HEADS_PER_LANE_TILE = V7X_LANES // SSM_HEAD_DIM
CONV_HALO = 8


def _nbytes(shape, dtype):
    return math.prod(shape) * jnp.dtype(dtype).itemsize


def _params(semantics, blocks, scratch=(), temps=0):
    need = 2 * sum(_nbytes(s, d) for s, d in blocks)
    need += sum(_nbytes(s, d) for s, d in scratch) + temps
    limit = min(V7X_VMEM_BYTES - (4 << 20), max(32 << 20, need + (8 << 20)))
    return pltpu.CompilerParams(dimension_semantics=semantics, vmem_limit_bytes=limit)


def _rmsnorm_rows(x, gain):
    ms = jnp.mean(x * x, axis=-1, keepdims=True)
    return x * lax.rsqrt(ms + EPS) * gain


def _ffn_kernel(x_ref, g_ref, wg_ref, wu_ref, wo_ref, o_ref, h_ref):
    j = pl.program_id(1)

    @pl.when(j == 0)
    def _():
        x = x_ref[...]
        h_ref[...] = _rmsnorm_rows(x, g_ref[...]).astype(BF16)
        o_ref[...] = x

    h = h_ref[...]
    gate = jnp.dot(h, wg_ref[...].astype(BF16), preferred_element_type=F32)
    up = jnp.dot(h, wu_ref[...].astype(BF16), preferred_element_type=F32)
    act = (gate * jax.nn.sigmoid(gate) * 0.5 * up).astype(BF16)
    o_ref[...] += jnp.dot(act, wo_ref[...].astype(BF16), preferred_element_type=F32)


def _ffn(x, gain, w_in, w_out, *, tm, tf):
    n_tok, d = x.shape
    d_ff = w_out.shape[0]
    nf = d_ff // tf
    blocks = [((tm, d), F32), ((1, d), F32), ((d, tf), F32), ((d, tf), F32),
              ((tf, d), F32), ((tm, d), F32)]
    scratch = [((tm, d), BF16)]
    return pl.pallas_call(
        _ffn_kernel,
        grid=(n_tok // tm, nf),
        in_specs=[
            pl.BlockSpec((tm, d), lambda i, j: (i, 0)),
            pl.BlockSpec((1, d), lambda i, j: (0, 0)),
            pl.BlockSpec((d, tf), lambda i, j: (0, j)),
            pl.BlockSpec((d, tf), lambda i, j: (0, j + nf)),
            pl.BlockSpec((tf, d), lambda i, j: (j, 0)),
        ],
        out_specs=pl.BlockSpec((tm, d), lambda i, j: (i, 0)),
        out_shape=jax.ShapeDtypeStruct((n_tok, d), F32),
        scratch_shapes=[pltpu.VMEM(s, t) for s, t in scratch],
        compiler_params=_params(("parallel", "arbitrary"), blocks, scratch,
                                temps=4 * tm * tf * 4 + 3 * d * tf * 2),
        name="ffn",
    )(x, gain, w_in, w_in, w_out)


def _gm_in_kernel(x_ref, g_ref, w_ref, u_ref, v_ref, ssq_ref, h_ref, *, nu):
    j = pl.program_id(1)

    @pl.when(j == 0)
    def _():
        h_ref[...] = _rmsnorm_rows(x_ref[...], g_ref[...]).astype(BF16)
        ssq_ref[...] = jnp.zeros_like(ssq_ref)

    def gelu_tile():
        z = jnp.dot(h_ref[...], w_ref[...].astype(BF16), preferred_element_type=F32)
        return 0.5 * z * (1.0 + lax.erf(z * math.sqrt(0.5)))

    @pl.when(j < nu)
    def _():
        u_ref[...] = gelu_tile()

    @pl.when(j >= nu)
    def _():
        v = gelu_tile()
        v_ref[...] = v.astype(BF16)
        ssq_ref[...] += jnp.sum(v * v, axis=-1, keepdims=True)


def _gm_in(x, gain, w_in, *, tm, tn):
    n_tok, d = x.shape
    inner = w_in.shape[1] // 2
    nu = inner // tn
    blocks = [((tm, d), F32), ((1, d), F32), ((d, tn), F32), ((tm, tn), F32), ((tm, tn), BF16),
              ((tm, V7X_LANES), F32)]
    scratch = [((tm, d), BF16)]
    return pl.pallas_call(
        functools.partial(_gm_in_kernel, nu=nu),
        grid=(n_tok // tm, 2 * nu),
        in_specs=[
            pl.BlockSpec((tm, d), lambda i, j: (i, 0)),
            pl.BlockSpec((1, d), lambda i, j: (0, 0)),
            pl.BlockSpec((d, tn), lambda i, j: (0, j)),
        ],
        out_specs=[
            pl.BlockSpec((tm, tn), lambda i, j: (i, jnp.minimum(j, nu - 1))),
            pl.BlockSpec((tm, tn), lambda i, j: (i, jnp.maximum(j - nu, 0))),
            pl.BlockSpec((tm, 1), lambda i, j: (i, 0)),
        ],
        out_shape=[jax.ShapeDtypeStruct((n_tok, inner), F32),
                   jax.ShapeDtypeStruct((n_tok, inner), BF16),
                   jax.ShapeDtypeStruct((n_tok, 1), F32)],
        scratch_shapes=[pltpu.VMEM(s, t) for s, t in scratch],
        compiler_params=_params(("parallel", "arbitrary"), blocks, scratch,
                                temps=4 * tm * tn * 4 + d * tn * 2),
        name="gm_in",
    )(x, gain, w_in)


def _gm_out_kernel(x_ref, u_ref, v_ref, ssq_ref, vg_ref, ws_ref, bs_ref, wo_ref, o_ref,
                   gated_ref, *, inner):
    j = pl.program_id(1)

    @pl.when(j == 0)
    def _():
        o_ref[...] = x_ref[...]

    groups = ws_ref.shape[0]
    gw = v_ref.shape[1] // groups
    tm = v_ref.shape[0]
    t_idx = lax.broadcasted_iota(jnp.int32, (GM_CHUNK, GM_CHUNK), 0)
    s_idx = lax.broadcasted_iota(jnp.int32, (GM_CHUNK, GM_CHUNK), 1)
    causal = s_idx <= t_idx
    scale = lax.rsqrt(ssq_ref[...] / inner + EPS)
    for g in range(groups):
        w = jnp.where(causal, ws_ref[g], 0.0)
        b = bs_ref[g]
        cols = slice(g * gw, (g + 1) * gw)
        gain = vg_ref[:, cols]
        for c in range(tm // GM_CHUNK):
            rows = slice(c * GM_CHUNK, (c + 1) * GM_CHUNK)
            w_c = (w * scale[:, rows]).astype(BF16)
            mixed = jnp.dot(w_c, v_ref[rows, cols], preferred_element_type=F32) * gain + b
            gated_ref[rows, cols] = (u_ref[rows, cols] * mixed).astype(BF16)
    o_ref[...] += jnp.dot(gated_ref[...], wo_ref[...].astype(BF16),
                          preferred_element_type=F32)


def _gm_out(x, u, v, ssq_row, v_gain, w_s, b_s, w_out, *, tm, tc):
    n_tok, d = x.shape
    inner = u.shape[1]
    gw = inner // GM_GROUPS
    gpt = tc // gw
    blocks = [((tm, d), F32), ((tm, tc), F32), ((tm, tc), BF16), ((8, tm), F32), ((8, tc), F32),
              ((gpt, GM_CHUNK, GM_CHUNK), F32), ((gpt, GM_CHUNK, V7X_LANES), F32),
              ((tc, d), F32), ((tm, d), F32)]
    scratch = [((tm, tc), BF16)]
    return pl.pallas_call(
        functools.partial(_gm_out_kernel, inner=inner),
        grid=(n_tok // tm, inner // tc),
        in_specs=[
            pl.BlockSpec((tm, d), lambda i, j: (i, 0)),
            pl.BlockSpec((tm, tc), lambda i, j: (i, j)),
            pl.BlockSpec((tm, tc), lambda i, j: (i, j)),
            pl.BlockSpec((1, tm), lambda i, j: (0, i)),
            pl.BlockSpec((1, tc), lambda i, j: (0, j)),
            pl.BlockSpec((gpt, GM_CHUNK, GM_CHUNK), lambda i, j: (j, 0, 0)),
            pl.BlockSpec((gpt, GM_CHUNK, 1), lambda i, j: (j, 0, 0)),
            pl.BlockSpec((tc, d), lambda i, j: (j, 0)),
        ],
        out_specs=pl.BlockSpec((tm, d), lambda i, j: (i, 0)),
        out_shape=jax.ShapeDtypeStruct((n_tok, d), F32),
        scratch_shapes=[pltpu.VMEM(s, t) for s, t in scratch],
        compiler_params=_params(("parallel", "arbitrary"), blocks, scratch,
                                temps=2 * tm * tc * 4 + tc * d * 2),
        name="gm_out",
    )(x, u, v, ssq_row, v_gain, w_s, b_s[..., None], w_out)


def _ssd_in_kernel(x_ref, g_ref, w_ref, wdt_ref, cw_ref, cb_ref, zx_ref, dt_ref,
                   h_ref, pad_ref, carry_ref, *, tiles_per_seq, nz):
    i = pl.program_id(0)
    j = pl.program_id(1)
    tm = zx_ref.shape[0]
    seq_start = (i % tiles_per_seq) == 0

    @pl.when(j == 0)
    def _():
        h_ref[...] = _rmsnorm_rows(x_ref[...], g_ref[...]).astype(BF16)
        dt_ref[...] = jnp.dot(h_ref[...], wdt_ref[...], preferred_element_type=F32)

    @pl.when(j < nz)
    def _():
        acc = jnp.dot(h_ref[...], w_ref[...].astype(BF16), preferred_element_type=F32)
        zx_ref[...] = acc * jax.nn.sigmoid(acc)

    @pl.when(j >= nz)
    def _():
        t = j - nz

        @pl.when(seq_start)
        def _():
            pad_ref[0:CONV_HALO, :] = jnp.zeros((CONV_HALO, pad_ref.shape[1]), F32)

        @pl.when(jnp.logical_not(seq_start))
        def _():
            pad_ref[0:CONV_HALO, :] = carry_ref[t]

        acc = jnp.dot(h_ref[...], w_ref[...].astype(BF16), preferred_element_type=F32)
        pad_ref[CONV_HALO:CONV_HALO + tm, :] = acc
        carry_ref[t] = acc[tm - CONV_HALO:tm, :]
        out = cb_ref[...] + cw_ref[SSM_CONV - 1:SSM_CONV, :] * acc
        for k in range(SSM_CONV - 1):
            off = CONV_HALO - (SSM_CONV - 1) + k
            out += cw_ref[k:k + 1, :] * pad_ref[off:off + tm, :]
        zx_ref[...] = out * jax.nn.sigmoid(out)


def _ssd_in(x, gain, w_in, w_dt, conv_w_t, conv_b, *, n_out, tm, tn, seq):
    n_tok, d = x.shape
    nj = n_out // tn
    nz = nj - conv_w_t.shape[1] // tn
    blocks = [((tm, d), F32), ((1, d), F32), ((d, tn), F32), ((d, V7X_LANES), BF16),
              ((8, tn), F32), ((8, tn), F32), ((tm, tn), F32), ((tm, V7X_LANES), F32)]
    scratch = [((tm, d), BF16), ((tm + CONV_HALO, tn), F32), ((nj - nz, CONV_HALO, tn), F32)]
    return pl.pallas_call(
        functools.partial(_ssd_in_kernel, tiles_per_seq=seq // tm, nz=nz),
        grid=(n_tok // tm, nj),
        in_specs=[
            pl.BlockSpec((tm, d), lambda i, j: (i, 0)),
            pl.BlockSpec((1, d), lambda i, j: (0, 0)),
            pl.BlockSpec((d, tn), lambda i, j: (0, j)),
            pl.BlockSpec((d, V7X_LANES), lambda i, j: (0, 0)),
            pl.BlockSpec((SSM_CONV, tn), lambda i, j: (0, jnp.maximum(j - nz, 0))),
            pl.BlockSpec((1, tn), lambda i, j: (0, jnp.maximum(j - nz, 0))),
        ],
        out_specs=[
            pl.BlockSpec((tm, tn), lambda i, j: (i, j)),
            pl.BlockSpec((tm, V7X_LANES), lambda i, j: (i, 0)),
        ],
        out_shape=[jax.ShapeDtypeStruct((n_tok, n_out), F32),
                   jax.ShapeDtypeStruct((n_tok, V7X_LANES), F32)],
        scratch_shapes=[pltpu.VMEM(s, t) for s, t in scratch],
        compiler_params=_params(("arbitrary", "arbitrary"), blocks, scratch,
                                temps=3 * tm * tn * 4 + d * tn * 2),
        name="ssd_in",
    )(x, gain, w_in, w_dt, conv_w_t, conv_b)


def _ssd_core_kernel(zs_ref, xs_ref, b_ref, c_ref, dtt_ref, dtb_ref, al_ref, dsk_ref, ng_ref,
                     y_ref, state_ref):
    L = SSM_CHUNK
    P = SSM_HEAD_DIM
    N = SSM_STATE
    hpg = SSM_HEADS_PER_GROUP
    gw = SSM_GROUP_WIDTH
    heads = dtt_ref.shape[0]

    @pl.when(pl.program_id(1) == 0)
    def _():
        state_ref[...] = jnp.zeros_like(state_ref)

    dtt = jax.nn.softplus(dtt_ref[...] + dtb_ref[...])
    at = -jnp.exp(al_ref[...])
    r_idx = lax.broadcasted_iota(jnp.int32, (L, L), 0)
    c_idx = lax.broadcasted_iota(jnp.int32, (L, L), 1)
    causal = c_idx <= r_idx
    upper = (r_idx <= c_idx).astype(F32)
    a_cum_t = jnp.dot(dtt * at, upper, preferred_element_type=F32,
                      precision=lax.Precision.HIGHEST)
    last_t = a_cum_t[:, L - 1:L]
    w_t = dtt * jnp.exp(last_t - a_cum_t)
    dk_b = jnp.exp(jnp.broadcast_to(last_t, (heads, N)))
    stack_t = jnp.concatenate(
        [a_cum_t, jnp.zeros((V7X_LANES - heads, L), F32)], axis=0).T
    low_half = lax.broadcasted_iota(jnp.int32, (L, V7X_LANES), 1) < P

    for g in range(SSM_GROUPS):
        gcols = slice(g * gw, (g + 1) * gw)
        bm = b_ref[:, g * N:(g + 1) * N].astype(BF16)
        cm = c_ref[:, g * N:(g + 1) * N].astype(BF16)
        xg = xs_ref[:, gcols]
        st = state_ref[gcols, :]
        hs = range(g * hpg, (g + 1) * hpg)

        cb = lax.dot_general(cm, bm, (((1,), (1,)), ((), ())),
                             preferred_element_type=F32)
        y_off = lax.dot_general(cm, st.astype(BF16), (((1,), (1,)), ((), ())),
                                preferred_element_type=F32)

        w_rows = jnp.concatenate(
            [jnp.broadcast_to(w_t[h:h + 1, :], (P, L)) for h in hs], axis=0)
        dk_rows = jnp.concatenate(
            [jnp.broadcast_to(dk_b[h:h + 1, :], (P, N)) for h in hs], axis=0)
        xw_t = (xg.T * w_rows).astype(BF16)
        state_ref[gcols, :] = st * dk_rows + jnp.dot(xw_t, bm, preferred_element_type=F32)

        y_tiles = []
        for q in range(hpg // HEADS_PER_LANE_TILE):
            qcols = slice(q * V7X_LANES, (q + 1) * V7X_LANES)
            x_q = xg[:, qcols]
            a_cols = []
            scores = []
            for hh in range(HEADS_PER_LANE_TILE):
                h = g * hpg + q * HEADS_PER_LANE_TILE + hh
                a_col = jnp.broadcast_to(stack_t[:, h:h + 1], (L, L))
                seg = a_col - a_cum_t[h:h + 1, :]
                decay = jnp.exp(jnp.where(causal, seg, -jnp.inf))
                scores.append((cb * decay * dtt[h:h + 1, :]).astype(BF16))
                a_cols.append(a_col)
            x_lo = jnp.where(low_half, x_q, 0.0).astype(BF16)
            x_hi = jnp.where(low_half, 0.0, x_q).astype(BF16)
            y_q = jnp.dot(jnp.concatenate(scores, axis=1),
                          jnp.concatenate([x_lo, x_hi], axis=0),
                          preferred_element_type=F32)
            e_q = jnp.exp(jnp.where(low_half, a_cols[0], a_cols[1]))
            y_tiles.append(y_q + y_off[:, qcols] * e_q)

        y = jnp.concatenate(y_tiles, axis=1) + xg * dsk_ref[:, gcols]
        y = y * zs_ref[:, gcols]
        y = y * lax.rsqrt(jnp.mean(y * y, axis=-1, keepdims=True) + EPS)
        y_ref[:, gcols] = (y * ng_ref[:, gcols]).astype(BF16)


def _ssd_core(zx, dtt, dt_bias, a_log, d_skip, norm_g, *, batch, seq):
    n_tok = zx.shape[0]
    inner = norm_g.shape[1]
    heads = dtt.shape[0]
    nc = seq // SSM_CHUNK
    bc_w = SSM_GROUPS * SSM_STATE
    tok = lambda b, c: b * nc + c
    col = lambda v: v.reshape(-1, 1)
    d_lanes = jnp.repeat(d_skip, SSM_HEAD_DIM).reshape(1, inner)
    blocks = [((SSM_CHUNK, inner), F32)] * 2 + [((SSM_CHUNK, bc_w), F32)] * 2 + \
             [((heads, SSM_CHUNK), F32), ((SSM_CHUNK, inner), BF16)]
    scratch = [((inner, SSM_STATE), F32)]
    return pl.pallas_call(
        _ssd_core_kernel,
        grid=(batch, nc),
        in_specs=[
            pl.BlockSpec((SSM_CHUNK, inner), lambda b, c: (tok(b, c), 0)),
            pl.BlockSpec((SSM_CHUNK, inner), lambda b, c: (tok(b, c), 1)),
            pl.BlockSpec((SSM_CHUNK, bc_w), lambda b, c: (tok(b, c), 2 * inner // bc_w)),
            pl.BlockSpec((SSM_CHUNK, bc_w), lambda b, c: (tok(b, c), 2 * inner // bc_w + 1)),
            pl.BlockSpec((heads, SSM_CHUNK), lambda b, c: (0, tok(b, c))),
            pl.BlockSpec((heads, 1), lambda b, c: (0, 0)),
            pl.BlockSpec((heads, 1), lambda b, c: (0, 0)),
            pl.BlockSpec((1, inner), lambda b, c: (0, 0)),
            pl.BlockSpec((1, inner), lambda b, c: (0, 0)),
        ],
        out_specs=pl.BlockSpec((SSM_CHUNK, inner), lambda b, c: (tok(b, c), 0)),
        out_shape=jax.ShapeDtypeStruct((n_tok, inner), BF16),
        scratch_shapes=[pltpu.VMEM(s, t) for s, t in scratch],
        compiler_params=_params(("parallel", "arbitrary"), blocks, scratch, temps=8 << 20),
        name="ssd_core",
    )(zx, zx, zx, zx, dtt, col(dt_bias), col(a_log), d_lanes, norm_g)


def _proj_res_kernel(y_ref, w_ref, x_ref, o_ref, wb_ref):
    @pl.when(pl.program_id(1) == 0)
    def _():
        wb_ref[...] = w_ref[...].astype(BF16)

    o_ref[...] = x_ref[...] + jnp.dot(y_ref[...], wb_ref[...], preferred_element_type=F32)


def _proj_res(y, w, x, *, tm, tn):
    n_tok, k = y.shape
    d = w.shape[1]
    blocks = [((tm, k), BF16), ((k, tn), F32), ((tm, tn), F32), ((tm, tn), F32)]
    scratch = [((k, tn), BF16)]
    return pl.pallas_call(
        _proj_res_kernel,
        grid=(d // tn, n_tok // tm),
        in_specs=[
            pl.BlockSpec((tm, k), lambda j, i: (i, 0)),
            pl.BlockSpec((k, tn), lambda j, i: (0, j)),
            pl.BlockSpec((tm, tn), lambda j, i: (i, j)),
        ],
        out_specs=pl.BlockSpec((tm, tn), lambda j, i: (i, j)),
        out_shape=jax.ShapeDtypeStruct((n_tok, d), F32),
        scratch_shapes=[pltpu.VMEM(s, t) for s, t in scratch],
        compiler_params=_params(("parallel", "arbitrary"), blocks, scratch, temps=tm * tn * 4),
        name="proj_res",
    )(y, w, x)


def _final_norm_kernel(x_ref, g_ref, o_ref):
    o_ref[...] = _rmsnorm_rows(x_ref[...], g_ref[...])


def _final_norm(x, gain, *, tm):
    n_tok, d = x.shape
    blocks = [((tm, d), F32), ((1, d), F32), ((tm, d), F32)]
    return pl.pallas_call(
        _final_norm_kernel,
        grid=(n_tok // tm,),
        in_specs=[pl.BlockSpec((tm, d), lambda i: (i, 0)),
                  pl.BlockSpec((1, d), lambda i: (0, 0))],
        out_specs=pl.BlockSpec((tm, d), lambda i: (i, 0)),
        out_shape=jax.ShapeDtypeStruct((n_tok, d), F32),
        compiler_params=_params(("parallel",), blocks, temps=2 * tm * d * 4),
        name="final_norm",
    )(x, gain)


def kernel(x, ln_ffn_pre, ffn_pre_w_in, ffn_pre_w_out, ln_mix, ln_ffn_post, ffn_post_w_in,
           ffn_post_w_out, gm_w_in, gm_v_norm, gm_w_s, gm_b_s, gm_w_out, ssm_w_in, ssm_conv_w,
           ssm_conv_b, ssm_dt_bias, ssm_a_log, ssm_d, ssm_norm, ssm_w_out, ln_final):
    batch, seq, d = x.shape
    depth = ln_mix.shape[0]
    n_tok = batch * seq
    inner = ssm_norm.shape[1]
    heads = ssm_dt_bias.shape[1]
    zx_cols = ssm_conv_w.shape[1] + inner
    assert seq % SSM_CHUNK == 0 and seq % GM_CHUNK == 0
    assert heads == SSM_GROUPS * SSM_HEADS_PER_GROUP and inner == heads * SSM_HEAD_DIM

    row = lambda v: v.reshape(1, -1)
    xf = x.reshape(n_tok, d)
    for i in range(depth):
        xf = _ffn(xf, row(ln_ffn_pre[i]), ffn_pre_w_in[i], ffn_pre_w_out[i], tm=1024, tf=256)
        m = i // 2
        if i % 2 == 0:
            u, v, ssq = _gm_in(xf, row(ln_mix[i]), gm_w_in[m], tm=1024, tn=512)
            xf = _gm_out(xf, u, v, ssq.reshape(1, n_tok), row(gm_v_norm[m]), gm_w_s[m], gm_b_s[m],
                         gm_w_out[m], tm=1024, tc=512)
        else:
            w_in = ssm_w_in[m]
            w_dt = jnp.pad(w_in[:, zx_cols:], ((0, 0), (0, V7X_LANES - heads))).astype(BF16)
            zx, dt = _ssd_in(xf, row(ln_mix[i]), w_in, w_dt, ssm_conv_w[m].T,
                             row(ssm_conv_b[m]), n_out=zx_cols, tm=1024, tn=512, seq=seq)
            dtt = dt[:, :heads].T
            y = _ssd_core(zx, dtt, ssm_dt_bias[m], ssm_a_log[m], ssm_d[m], row(ssm_norm[m]),
                          batch=batch, seq=seq)
            xf = _proj_res(y, ssm_w_out[m], xf, tm=1024, tn=512)
        xf = _ffn(xf, row(ln_ffn_post[i]), ffn_post_w_in[i], ffn_post_w_out[i], tm=1024, tf=256)
    out = _final_norm(xf, row(ln_final), tm=512)
    return out.reshape(batch, seq, d)
```

```python
import functools
import math

import jax
import jax.numpy as jnp
from jax import lax
from jax.experimental import pallas as pl
from jax.experimental.pallas import tpu as pltpu

F32 = jnp.float32
BF16 = jnp.bfloat16
EPS = 1e-6

V7X_VMEM_BYTES = 64 * 1024 * 1024
V7X_LANES = 128

GM_CHUNK = 128
GM_GROUPS = 16
SSM_HEAD_DIM = 64
SSM_GROUPS = 8
SSM_HEADS_PER_GROUP = 8
SSM_STATE = 128
SSM_CONV = 4
SSM_CHUNK = 128
SSM_GROUP_WIDTH = SSM_HEADS_PER_GROUP * SSM_HEAD_DIM
HEADS_PER_LANE_TILE = V7X_LANES // SSM_HEAD_DIM
CONV_HALO = 8


def _nbytes(shape, dtype):
    return math.prod(shape) * jnp.dtype(dtype).itemsize


def _params(semantics, blocks, scratch=(), temps=0):
    need = 2 * sum(_nbytes(s, d) for s, d in blocks)
    need += sum(_nbytes(s, d) for s, d in scratch) + temps
    limit = min(V7X_VMEM_BYTES - (4 << 20), max(32 << 20, need + (8 << 20)))
    return pltpu.CompilerParams(dimension_semantics=semantics, vmem_limit_bytes=limit)


_NT = (((1,), (1,)), ((), ()))


def _rmsnorm_rows(x, gain):
    ms = jnp.mean(x * x, axis=-1, keepdims=True)
    return x * lax.rsqrt(ms + EPS) * gain


def _ffn_kernel(x_ref, g_ref, wg_ref, wu_ref, wo_ref, o_ref, h_ref):
    j = pl.program_id(1)

    @pl.when(j == 0)
    def _():
        x = x_ref[...]
        h_ref[...] = _rmsnorm_rows(x, g_ref[...]).astype(BF16)
        o_ref[...] = x

    h = h_ref[...]
    gate = jnp.dot(h, wg_ref[...].astype(BF16), preferred_element_type=F32)
    up = jnp.dot(h, wu_ref[...].astype(BF16), preferred_element_type=F32)
    act = (gate * jax.nn.sigmoid(gate) * 0.5 * up).astype(BF16)
    o_ref[...] += jnp.dot(act, wo_ref[...].astype(BF16), preferred_element_type=F32)


def _ffn(x, gain, w_in, w_out, *, layer, tm, tf):
    n_tok, d = x.shape
    d_ff = w_out.shape[1]
    nf = d_ff // tf
    blocks = [((tm, d), F32), ((1, d), F32), ((d, tf), F32), ((d, tf), F32),
              ((tf, d), F32), ((tm, d), F32)]
    scratch = [((tm, d), BF16)]
    return pl.pallas_call(
        _ffn_kernel,
        grid=(n_tok // tm, nf),
        in_specs=[
            pl.BlockSpec((tm, d), lambda i, j: (i, 0)),
            pl.BlockSpec((1, d), lambda i, j: (0, 0)),
            pl.BlockSpec((None, d, tf), lambda i, j: (layer, 0, j)),
            pl.BlockSpec((None, d, tf), lambda i, j: (layer, 0, j + nf)),
            pl.BlockSpec((None, tf, d), lambda i, j: (layer, j, 0)),
        ],
        out_specs=pl.BlockSpec((tm, d), lambda i, j: (i, 0)),
        out_shape=jax.ShapeDtypeStruct((n_tok, d), F32),
        scratch_shapes=[pltpu.VMEM(s, t) for s, t in scratch],
        compiler_params=_params(("parallel", "arbitrary"), blocks, scratch,
                                temps=4 * tm * tf * 4 + 3 * d * tf * 2),
        name="ffn",
    )(x, gain, w_in, w_in, w_out)


def _gm_in_kernel(x_ref, g_ref, w_ref, u_ref, v_ref, ssq_ref, h_ref, *, nu):
    j = pl.program_id(1)

    @pl.when(j == 0)
    def _():
        h_ref[...] = _rmsnorm_rows(x_ref[...], g_ref[...]).astype(BF16)
        ssq_ref[...] = jnp.zeros_like(ssq_ref)

    def gelu_tile():
        z = jnp.dot(h_ref[...], w_ref[...].astype(BF16), preferred_element_type=F32)
        return 0.5 * z * (1.0 + lax.erf(z * math.sqrt(0.5)))

    @pl.when(j < nu)
    def _():
        u_ref[...] = gelu_tile()

    @pl.when(j >= nu)
    def _():
        v = gelu_tile()
        v_ref[...] = v.astype(BF16)
        ssq_ref[...] += jnp.sum(v * v, axis=-1, keepdims=True)


def _gm_in(x, gain, w_in, *, layer, tm, tn):
    n_tok, d = x.shape
    inner = w_in.shape[2] // 2
    nu = inner // tn
    blocks = [((tm, d), F32), ((1, d), F32), ((d, tn), F32), ((tm, tn), F32), ((tm, tn), BF16),
              ((tm, V7X_LANES), F32)]
    scratch = [((tm, d), BF16)]
    return pl.pallas_call(
        functools.partial(_gm_in_kernel, nu=nu),
        grid=(n_tok // tm, 2 * nu),
        in_specs=[
            pl.BlockSpec((tm, d), lambda i, j: (i, 0)),
            pl.BlockSpec((1, d), lambda i, j: (0, 0)),
            pl.BlockSpec((None, d, tn), lambda i, j: (layer, 0, j)),
        ],
        out_specs=[
            pl.BlockSpec((tm, tn), lambda i, j: (i, jnp.minimum(j, nu - 1))),
            pl.BlockSpec((tm, tn), lambda i, j: (i, jnp.maximum(j - nu, 0))),
            pl.BlockSpec((tm, 1), lambda i, j: (i, 0)),
        ],
        out_shape=[jax.ShapeDtypeStruct((n_tok, inner), F32),
                   jax.ShapeDtypeStruct((n_tok, inner), BF16),
                   jax.ShapeDtypeStruct((n_tok, 1), F32)],
        scratch_shapes=[pltpu.VMEM(s, t) for s, t in scratch],
        compiler_params=_params(("parallel", "arbitrary"), blocks, scratch,
                                temps=4 * tm * tn * 4 + d * tn * 2),
        name="gm_in",
    )(x, gain, w_in)


def _gm_out_kernel(x_ref, u_ref, v_ref, ssq_ref, vg_ref, ws_ref, bs_ref, wo_ref, o_ref,
                   gated_ref, *, inner):
    j = pl.program_id(1)

    @pl.when(j == 0)
    def _():
        o_ref[...] = x_ref[...]

    groups = ws_ref.shape[0]
    gw = v_ref.shape[1] // groups
    tm = v_ref.shape[0]
    t_idx = lax.broadcasted_iota(jnp.int32, (GM_CHUNK, GM_CHUNK), 0)
    s_idx = lax.broadcasted_iota(jnp.int32, (GM_CHUNK, GM_CHUNK), 1)
    causal = s_idx <= t_idx
    scale = lax.rsqrt(ssq_ref[...] / inner + EPS)
    for g in range(groups):
        w = jnp.where(causal, ws_ref[g], 0.0)
        b = bs_ref[g]
        cols = slice(g * gw, (g + 1) * gw)
        gain = vg_ref[:, cols]
        for c in range(tm // GM_CHUNK):
            rows = slice(c * GM_CHUNK, (c + 1) * GM_CHUNK)
            w_c = (w * scale[:, rows]).astype(BF16)
            mixed = jnp.dot(w_c, v_ref[rows, cols], preferred_element_type=F32) * gain + b
            gated_ref[rows, cols] = (u_ref[rows, cols] * mixed).astype(BF16)
    o_ref[...] += jnp.dot(gated_ref[...], wo_ref[...].astype(BF16),
                          preferred_element_type=F32)


def _gm_out(x, u, v, ssq_row, v_gain, w_s, b_s, w_out, *, layer, tm, tc):
    n_tok, d = x.shape
    inner = u.shape[1]
    gw = inner // GM_GROUPS
    gpt = tc // gw
    blocks = [((tm, d), F32), ((tm, tc), F32), ((tm, tc), BF16), ((8, tm), F32), ((8, tc), F32),
              ((gpt, GM_CHUNK, GM_CHUNK), F32), ((gpt, GM_CHUNK, V7X_LANES), F32),
              ((tc, d), F32), ((tm, d), F32)]
    scratch = [((tm, tc), BF16)]
    return pl.pallas_call(
        functools.partial(_gm_out_kernel, inner=inner),
        grid=(n_tok // tm, inner // tc),
        in_specs=[
            pl.BlockSpec((tm, d), lambda i, j: (i, 0)),
            pl.BlockSpec((tm, tc), lambda i, j: (i, j)),
            pl.BlockSpec((tm, tc), lambda i, j: (i, j)),
            pl.BlockSpec((1, tm), lambda i, j: (0, i)),
            pl.BlockSpec((1, tc), lambda i, j: (0, j)),
            pl.BlockSpec((gpt, GM_CHUNK, GM_CHUNK), lambda i, j: (j, 0, 0)),
            pl.BlockSpec((gpt, GM_CHUNK, 1), lambda i, j: (j, 0, 0)),
            pl.BlockSpec((None, tc, d), lambda i, j: (layer, j, 0)),
        ],
        out_specs=pl.BlockSpec((tm, d), lambda i, j: (i, 0)),
        out_shape=jax.ShapeDtypeStruct((n_tok, d), F32),
        scratch_shapes=[pltpu.VMEM(s, t) for s, t in scratch],
        compiler_params=_params(("parallel", "arbitrary"), blocks, scratch,
                                temps=2 * tm * tc * 4 + tc * d * 2),
        name="gm_out",
    )(x, u, v, ssq_row, v_gain, w_s, b_s[..., None], w_out)


def _ssd_in_kernel(x_ref, g_ref, w_ref, wdt_ref, cw_ref, cb_ref, zx_ref, dt_ref,
                   h_ref, pad_ref, carry_ref, *, tiles_per_seq, nz):
    i = pl.program_id(0)
    j = pl.program_id(1)
    tm = zx_ref.shape[0]
    seq_start = (i % tiles_per_seq) == 0

    @pl.when(j == 0)
    def _():
        h_ref[...] = _rmsnorm_rows(x_ref[...], g_ref[...]).astype(BF16)
        dt_ref[...] = lax.dot_general(wdt_ref[...].astype(BF16), h_ref[...], _NT,
                                      preferred_element_type=F32)

    @pl.when(j < nz)
    def _():
        acc = lax.dot_general(h_ref[...], w_ref[...].astype(BF16), _NT,
                              preferred_element_type=F32)
        zx_ref[...] = acc * jax.nn.sigmoid(acc)

    @pl.when(j >= nz)
    def _():
        t = j - nz

        @pl.when(seq_start)
        def _():
            pad_ref[0:CONV_HALO, :] = jnp.zeros((CONV_HALO, pad_ref.shape[1]), F32)

        @pl.when(jnp.logical_not(seq_start))
        def _():
            pad_ref[0:CONV_HALO, :] = carry_ref[t]

        acc = lax.dot_general(h_ref[...], w_ref[...].astype(BF16), _NT,
                              preferred_element_type=F32)
        pad_ref[CONV_HALO:CONV_HALO + tm, :] = acc
        carry_ref[t] = acc[tm - CONV_HALO:tm, :]
        out = cb_ref[...] + cw_ref[SSM_CONV - 1:SSM_CONV, :] * acc
        for k in range(SSM_CONV - 1):
            off = CONV_HALO - (SSM_CONV - 1) + k
            out += cw_ref[k:k + 1, :] * pad_ref[off:off + tm, :]
        zx_ref[...] = out * jax.nn.sigmoid(out)


def _ssd_in(x, gain, w_in_t, conv_w_t, conv_b, *, layer, n_out, tm, tn, seq):
    n_tok, d = x.shape
    heads = w_in_t.shape[1] - n_out
    nj = n_out // tn
    nz = nj - conv_w_t.shape[1] // tn
    blocks = [((tm, d), F32), ((1, d), F32), ((tn, d), F32), ((heads, d), F32),
              ((8, tn), F32), ((8, tn), F32), ((tm, tn), F32), ((heads, tm), F32)]
    scratch = [((tm, d), BF16), ((tm + CONV_HALO, tn), F32), ((nj - nz, CONV_HALO, tn), F32)]
    return pl.pallas_call(
        functools.partial(_ssd_in_kernel, tiles_per_seq=seq // tm, nz=nz),
        grid=(n_tok // tm, nj),
        in_specs=[
            pl.BlockSpec((tm, d), lambda i, j: (i, 0)),
            pl.BlockSpec((1, d), lambda i, j: (0, 0)),
            pl.BlockSpec((None, tn, d), lambda i, j: (layer, j, 0)),
            pl.BlockSpec((None, heads, d), lambda i, j: (layer, n_out // heads, 0)),
            pl.BlockSpec((SSM_CONV, tn), lambda i, j: (0, jnp.maximum(j - nz, 0))),
            pl.BlockSpec((1, tn), lambda i, j: (0, jnp.maximum(j - nz, 0))),
        ],
        out_specs=[
            pl.BlockSpec((tm, tn), lambda i, j: (i, j)),
            pl.BlockSpec((heads, tm), lambda i, j: (0, i)),
        ],
        out_shape=[jax.ShapeDtypeStruct((n_tok, n_out), F32),
                   jax.ShapeDtypeStruct((heads, n_tok), F32)],
        scratch_shapes=[pltpu.VMEM(s, t) for s, t in scratch],
        compiler_params=_params(("arbitrary", "arbitrary"), blocks, scratch,
                                temps=3 * tm * tn * 4 + d * tn * 2),
        name="ssd_in",
    )(x, gain, w_in_t, w_in_t, conv_w_t, conv_b)


def _ssd_core_kernel(zs_ref, xs_ref, b_ref, c_ref, dtt_ref, dtb_ref, al_ref, dsk_ref, ng_ref,
                     y_ref, state_ref):
    L = SSM_CHUNK
    P = SSM_HEAD_DIM
    N = SSM_STATE
    hpg = SSM_HEADS_PER_GROUP
    gw = SSM_GROUP_WIDTH
    heads = dtt_ref.shape[0]

    @pl.when(pl.program_id(1) == 0)
    def _():
        state_ref[...] = jnp.zeros_like(state_ref)

    dtt = jax.nn.softplus(dtt_ref[...] + dtb_ref[...])
    at = -jnp.exp(al_ref[...])
    r_idx = lax.broadcasted_iota(jnp.int32, (L, L), 0)
    c_idx = lax.broadcasted_iota(jnp.int32, (L, L), 1)
    causal = c_idx <= r_idx
    upper = (r_idx <= c_idx).astype(F32)
    a_cum_t = jnp.dot(dtt * at, upper, preferred_element_type=F32,
                      precision=lax.Precision.HIGHEST)
    last_t = a_cum_t[:, L - 1:L]
    w_t = dtt * jnp.exp(last_t - a_cum_t)
    dk_b = jnp.exp(jnp.broadcast_to(last_t, (heads, N)))
    stack_t = jnp.concatenate(
        [a_cum_t, jnp.zeros((V7X_LANES - heads, L), F32)], axis=0).T
    low_half = lax.broadcasted_iota(jnp.int32, (L, V7X_LANES), 1) < P

    for g in range(SSM_GROUPS):
        gcols = slice(g * gw, (g + 1) * gw)
        bm = b_ref[:, g * N:(g + 1) * N].astype(BF16)
        cm = c_ref[:, g * N:(g + 1) * N].astype(BF16)
        xg = xs_ref[:, gcols]
        st = state_ref[gcols, :]
        hs = range(g * hpg, (g + 1) * hpg)

        cb = lax.dot_general(cm, bm, (((1,), (1,)), ((), ())),
                             preferred_element_type=F32)
        y_off = lax.dot_general(cm, st.astype(BF16), (((1,), (1,)), ((), ())),
                                preferred_element_type=F32)

        w_rows = jnp.concatenate(
            [jnp.broadcast_to(w_t[h:h + 1, :], (P, L)) for h in hs], axis=0)
        dk_rows = jnp.concatenate(
            [jnp.broadcast_to(dk_b[h:h + 1, :], (P, N)) for h in hs], axis=0)
        xw_t = (xg.T * w_rows).astype(BF16)
        state_ref[gcols, :] = st * dk_rows + jnp.dot(xw_t, bm, preferred_element_type=F32)

        y_tiles = []
        for q in range(hpg // HEADS_PER_LANE_TILE):
            qcols = slice(q * V7X_LANES, (q + 1) * V7X_LANES)
            x_q = xg[:, qcols]
            a_cols = []
            scores = []
            for hh in range(HEADS_PER_LANE_TILE):
                h = g * hpg + q * HEADS_PER_LANE_TILE + hh
                a_col = jnp.broadcast_to(stack_t[:, h:h + 1], (L, L))
                seg = a_col - a_cum_t[h:h + 1, :]
                decay = jnp.exp(jnp.where(causal, seg, -jnp.inf))
                scores.append((cb * decay * dtt[h:h + 1, :]).astype(BF16))
                a_cols.append(a_col)
            x_lo = jnp.where(low_half, x_q, 0.0).astype(BF16)
            x_hi = jnp.where(low_half, 0.0, x_q).astype(BF16)
            y_q = jnp.dot(jnp.concatenate(scores, axis=1),
                          jnp.concatenate([x_lo, x_hi], axis=0),
                          preferred_element_type=F32)
            e_q = jnp.exp(jnp.where(low_half, a_cols[0], a_cols[1]))
            y_tiles.append(y_q + y_off[:, qcols] * e_q)

        y = jnp.concatenate(y_tiles, axis=1) + xg * dsk_ref[:, gcols]
        y = y * zs_ref[:, gcols]
        y = y * lax.rsqrt(jnp.mean(y * y, axis=-1, keepdims=True) + EPS)
        y_ref[:, gcols] = (y * ng_ref[:, gcols]).astype(BF16)


def _ssd_core(zx, dtt, dt_bias, a_log, d_skip, norm_g, *, batch, seq):
    n_tok = zx.shape[0]
    inner = norm_g.shape[1]
    heads = dtt.shape[0]
    nc = seq // SSM_CHUNK
    bc_w = SSM_GROUPS * SSM_STATE
    tok = lambda b, c: b * nc + c
    col = lambda v: v.reshape(-1, 1)
    d_lanes = jnp.repeat(d_skip, SSM_HEAD_DIM).reshape(1, inner)
    blocks = [((SSM_CHUNK, inner), F32)] * 2 + [((SSM_CHUNK, bc_w), F32)] * 2 + \
             [((heads, SSM_CHUNK), F32), ((SSM_CHUNK, inner), BF16)]
    scratch = [((inner, SSM_STATE), F32)]
    return pl.pallas_call(
        _ssd_core_kernel,
        grid=(batch, nc),
        in_specs=[
            pl.BlockSpec((SSM_CHUNK, inner), lambda b, c: (tok(b, c), 0)),
            pl.BlockSpec((SSM_CHUNK, inner), lambda b, c: (tok(b, c), 1)),
            pl.BlockSpec((SSM_CHUNK, bc_w), lambda b, c: (tok(b, c), 2 * inner // bc_w)),
            pl.BlockSpec((SSM_CHUNK, bc_w), lambda b, c: (tok(b, c), 2 * inner // bc_w + 1)),
            pl.BlockSpec((heads, SSM_CHUNK), lambda b, c: (0, tok(b, c))),
            pl.BlockSpec((heads, 1), lambda b, c: (0, 0)),
            pl.BlockSpec((heads, 1), lambda b, c: (0, 0)),
            pl.BlockSpec((1, inner), lambda b, c: (0, 0)),
            pl.BlockSpec((1, inner), lambda b, c: (0, 0)),
        ],
        out_specs=pl.BlockSpec((SSM_CHUNK, inner), lambda b, c: (tok(b, c), 0)),
        out_shape=jax.ShapeDtypeStruct((n_tok, inner), BF16),
        scratch_shapes=[pltpu.VMEM(s, t) for s, t in scratch],
        compiler_params=_params(("parallel", "arbitrary"), blocks, scratch, temps=8 << 20),
        name="ssd_core",
    )(zx, zx, zx, zx, dtt, col(dt_bias), col(a_log), d_lanes, norm_g)


def _proj_res_kernel(y_ref, w_ref, x_ref, o_ref, wb_ref):
    @pl.when(pl.program_id(1) == 0)
    def _():
        wb_ref[...] = w_ref[...].astype(BF16)

    o_ref[...] = x_ref[...] + jnp.dot(y_ref[...], wb_ref[...], preferred_element_type=F32)


def _proj_res(y, w, x, *, layer, tm, tn):
    n_tok, k = y.shape
    d = w.shape[2]
    blocks = [((tm, k), BF16), ((k, tn), F32), ((tm, tn), F32), ((tm, tn), F32)]
    scratch = [((k, tn), BF16)]
    return pl.pallas_call(
        _proj_res_kernel,
        grid=(d // tn, n_tok // tm),
        in_specs=[
            pl.BlockSpec((tm, k), lambda j, i: (i, 0)),
            pl.BlockSpec((None, k, tn), lambda j, i: (layer, 0, j)),
            pl.BlockSpec((tm, tn), lambda j, i: (i, j)),
        ],
        out_specs=pl.BlockSpec((tm, tn), lambda j, i: (i, j)),
        out_shape=jax.ShapeDtypeStruct((n_tok, d), F32),
        scratch_shapes=[pltpu.VMEM(s, t) for s, t in scratch],
        compiler_params=_params(("parallel", "arbitrary"), blocks, scratch, temps=tm * tn * 4),
        name="proj_res",
    )(y, w, x)


def _final_norm_kernel(x_ref, g_ref, o_ref):
    o_ref[...] = _rmsnorm_rows(x_ref[...], g_ref[...])


def _final_norm(x, gain, *, tm):
    n_tok, d = x.shape
    blocks = [((tm, d), F32), ((1, d), F32), ((tm, d), F32)]
    return pl.pallas_call(
        _final_norm_kernel,
        grid=(n_tok // tm,),
        in_specs=[pl.BlockSpec((tm, d), lambda i: (i, 0)),
                  pl.BlockSpec((1, d), lambda i: (0, 0))],
        out_specs=pl.BlockSpec((tm, d), lambda i: (i, 0)),
        out_shape=jax.ShapeDtypeStruct((n_tok, d), F32),
        compiler_params=_params(("parallel",), blocks, temps=2 * tm * d * 4),
        name="final_norm",
    )(x, gain)


def kernel(x, ln_ffn_pre, ffn_pre_w_in, ffn_pre_w_out, ln_mix, ln_ffn_post, ffn_post_w_in,
           ffn_post_w_out, gm_w_in, gm_v_norm, gm_w_s, gm_b_s, gm_w_out, ssm_w_in, ssm_conv_w,
           ssm_conv_b, ssm_dt_bias, ssm_a_log, ssm_d, ssm_norm, ssm_w_out, ln_final):
    batch, seq, d = x.shape
    depth = ln_mix.shape[0]
    n_tok = batch * seq
    inner = ssm_norm.shape[1]
    heads = ssm_dt_bias.shape[1]
    zx_cols = ssm_conv_w.shape[1] + inner
    assert seq % SSM_CHUNK == 0 and seq % GM_CHUNK == 0
    assert heads == SSM_GROUPS * SSM_HEADS_PER_GROUP and inner == heads * SSM_HEAD_DIM

    row = lambda v: v.reshape(1, -1)
    ssm_w_in_t = jnp.swapaxes(ssm_w_in, 1, 2)
    xf = x.reshape(n_tok, d)
    for i in range(depth):
        xf = _ffn(xf, row(ln_ffn_pre[i]), ffn_pre_w_in, ffn_pre_w_out, layer=i, tm=1024, tf=256)
        m = i // 2
        if i % 2 == 0:
            u, v, ssq = _gm_in(xf, row(ln_mix[i]), gm_w_in, layer=m, tm=1024, tn=512)
            xf = _gm_out(xf, u, v, ssq.reshape(1, n_tok), row(gm_v_norm[m]), gm_w_s[m], gm_b_s[m],
                         gm_w_out, layer=m, tm=1024, tc=512)
        else:
            zx, dtt = _ssd_in(xf, row(ln_mix[i]), ssm_w_in_t, ssm_conv_w[m].T,
                              row(ssm_conv_b[m]), layer=m, n_out=zx_cols, tm=1024, tn=512, seq=seq)
            y = _ssd_core(zx, dtt, ssm_dt_bias[m], ssm_a_log[m], ssm_d[m], row(ssm_norm[m]),
                          batch=batch, seq=seq)
            xf = _proj_res(y, ssm_w_out, xf, layer=m, tm=1024, tn=512)
        xf = _ffn(xf, row(ln_ffn_post[i]), ffn_post_w_in, ffn_post_w_out, layer=i, tm=1024, tf=256)
    out = _final_norm(xf, row(ln_final), tm=512)
    return out.reshape(batch, seq, d)
```

```python
import functools
import math

import jax
import jax.numpy as jnp
from jax import lax
from jax.experimental import pallas as pl
from jax.experimental.pallas import tpu as pltpu

F32 = jnp.float32
BF16 = jnp.bfloat16
EPS = 1e-6

V7X_VMEM_BYTES = 64 * 1024 * 1024
V7X_LANES = 128
V7X_MXU_COLS = 256
MXU_ROW_BLOCK = 512

GM_CHUNK = 128
GM_GROUPS = 16
SSM_HEAD_DIM = 64
SSM_GROUPS = 8
SSM_HEADS_PER_GROUP = 8
SSM_STATE = 128
SSM_CONV = 4
SSM_CHUNK = 128
SSM_GROUP_WIDTH = SSM_HEADS_PER_GROUP * SSM_HEAD_DIM
HEADS_PER_LANE_TILE = V7X_LANES // SSM_HEAD_DIM
CONV_HALO = 8


def _nbytes(shape, dtype):
    return math.prod(shape) * jnp.dtype(dtype).itemsize


def _params(semantics, blocks, scratch=(), temps=0):
    need = 2 * sum(_nbytes(s, d) for s, d in blocks)
    need += sum(_nbytes(s, d) for s, d in scratch) + temps
    limit = min(V7X_VMEM_BYTES - (4 << 20), max(32 << 20, need + (8 << 20)))
    return pltpu.CompilerParams(dimension_semantics=semantics, vmem_limit_bytes=limit)


_NT = (((1,), (1,)), ((), ()))


def _sub_tiles(n_cols, n_rows):
    cols = [slice(c, c + V7X_MXU_COLS) for c in range(0, n_cols, V7X_MXU_COLS)]
    rows = [slice(r, r + MXU_ROW_BLOCK) for r in range(0, n_rows, MXU_ROW_BLOCK)]
    return cols, rows


def _rmsnorm_rows(x, gain):
    ms = jnp.mean(x * x, axis=-1, keepdims=True)
    return x * lax.rsqrt(ms + EPS) * gain


def _ffn_kernel(x_ref, g_ref, wg_ref, wu_ref, wo_ref, o_ref, h_ref):
    j = pl.program_id(1)

    @pl.when(j == 0)
    def _():
        x = x_ref[...]
        h_ref[...] = _rmsnorm_rows(x, g_ref[...]).astype(BF16)
        o_ref[...] = x

    h = h_ref[...]
    gate = jnp.dot(h, wg_ref[...].astype(BF16), preferred_element_type=F32)
    up = jnp.dot(h, wu_ref[...].astype(BF16), preferred_element_type=F32)
    act = (gate * jax.nn.sigmoid(gate) * 0.5 * up).astype(BF16)
    o_ref[...] += jnp.dot(act, wo_ref[...].astype(BF16), preferred_element_type=F32)


def _ffn(x, gain, w_in, w_out, *, layer, tm, tf):
    n_tok, d = x.shape
    d_ff = w_out.shape[1]
    nf = d_ff // tf
    blocks = [((tm, d), F32), ((1, d), F32), ((d, tf), F32), ((d, tf), F32),
              ((tf, d), F32), ((tm, d), F32)]
    scratch = [((tm, d), BF16)]
    return pl.pallas_call(
        _ffn_kernel,
        grid=(n_tok // tm, nf),
        in_specs=[
            pl.BlockSpec((tm, d), lambda i, j: (i, 0), pipeline_mode=pl.Buffered(1)),
            pl.BlockSpec((1, d), lambda i, j: (0, 0)),
            pl.BlockSpec((None, d, tf), lambda i, j: (layer, 0, j)),
            pl.BlockSpec((None, d, tf), lambda i, j: (layer, 0, j + nf)),
            pl.BlockSpec((None, tf, d), lambda i, j: (layer, j, 0)),
        ],
        out_specs=pl.BlockSpec((tm, d), lambda i, j: (i, 0)),
        out_shape=jax.ShapeDtypeStruct((n_tok, d), F32),
        scratch_shapes=[pltpu.VMEM(s, t) for s, t in scratch],
        compiler_params=_params(("parallel", "arbitrary"), blocks, scratch,
                                temps=4 * tm * tf * 4 + 3 * d * tf * 2),
        name="ffn",
    )(x, gain, w_in, w_in, w_out)


def _gm_in_kernel(x_ref, g_ref, w_ref, u_ref, v_ref, ssq_ref, h_ref, *, nu):
    j = pl.program_id(1)

    @pl.when(j == 0)
    def _():
        h_ref[...] = _rmsnorm_rows(x_ref[...], g_ref[...]).astype(BF16)
        ssq_ref[...] = jnp.zeros_like(ssq_ref)

    def gelu_tile(rows, w):
        z = jnp.dot(h_ref[rows, :], w, preferred_element_type=F32)
        return 0.5 * z * (1.0 + lax.erf(z * math.sqrt(0.5)))

    col_tiles, row_tiles = _sub_tiles(w_ref.shape[1], h_ref.shape[0])

    @pl.when(j < nu)
    def _():
        for cols in col_tiles:
            w = w_ref[:, cols].astype(BF16)
            for rows in row_tiles:
                u_ref[rows, cols] = gelu_tile(rows, w)

    @pl.when(j >= nu)
    def _():
        ssq = [jnp.zeros((rows.stop - rows.start, 1), F32) for rows in row_tiles]
        for cols in col_tiles:
            w = w_ref[:, cols].astype(BF16)
            for r, rows in enumerate(row_tiles):
                v = gelu_tile(rows, w)
                v_ref[rows, cols] = v.astype(BF16)
                ssq[r] += jnp.sum(v * v, axis=-1, keepdims=True)
        for r, rows in enumerate(row_tiles):
            ssq_ref[rows, :] += ssq[r]


def _gm_in(x, gain, w_in, *, layer, tm, tn):
    n_tok, d = x.shape
    inner = w_in.shape[2] // 2
    nu = inner // tn
    blocks = [((tm, d), F32), ((1, d), F32), ((d, tn), F32), ((tm, tn), F32), ((tm, tn), BF16),
              ((tm, V7X_LANES), F32)]
    scratch = [((tm, d), BF16)]
    return pl.pallas_call(
        functools.partial(_gm_in_kernel, nu=nu),
        grid=(n_tok // tm, 2 * nu),
        in_specs=[
            pl.BlockSpec((tm, d), lambda i, j: (i, 0)),
            pl.BlockSpec((1, d), lambda i, j: (0, 0)),
            pl.BlockSpec((None, d, tn), lambda i, j: (layer, 0, j)),
        ],
        out_specs=[
            pl.BlockSpec((tm, tn), lambda i, j: (i, jnp.minimum(j, nu - 1))),
            pl.BlockSpec((tm, tn), lambda i, j: (i, jnp.maximum(j - nu, 0))),
            pl.BlockSpec((tm, 1), lambda i, j: (i, 0)),
        ],
        out_shape=[jax.ShapeDtypeStruct((n_tok, inner), F32),
                   jax.ShapeDtypeStruct((n_tok, inner), BF16),
                   jax.ShapeDtypeStruct((n_tok, 1), F32)],
        scratch_shapes=[pltpu.VMEM(s, t) for s, t in scratch],
        compiler_params=_params(("parallel", "arbitrary"), blocks, scratch,
                                temps=4 * tm * tn * 4 + d * tn * 2),
        name="gm_in",
    )(x, gain, w_in)


def _gm_out_kernel(x_ref, u_ref, v_ref, ssq_ref, vg_ref, ws_ref, bs_ref, wo_ref, o_ref,
                   gated_ref, *, inner):
    j = pl.program_id(1)

    @pl.when(j == 0)
    def _():
        o_ref[...] = x_ref[...]

    groups = ws_ref.shape[0]
    gw = v_ref.shape[1] // groups
    tm = v_ref.shape[0]
    t_idx = lax.broadcasted_iota(jnp.int32, (GM_CHUNK, GM_CHUNK), 0)
    s_idx = lax.broadcasted_iota(jnp.int32, (GM_CHUNK, GM_CHUNK), 1)
    causal = s_idx <= t_idx
    scale = lax.rsqrt(ssq_ref[...] / inner + EPS)
    for g in range(groups):
        w = jnp.where(causal, ws_ref[g], 0.0)
        b = bs_ref[g]
        cols = slice(g * gw, (g + 1) * gw)
        gain = vg_ref[:, cols]
        for c in range(tm // GM_CHUNK):
            rows = slice(c * GM_CHUNK, (c + 1) * GM_CHUNK)
            w_c = (w * scale[:, rows]).astype(BF16)
            mixed = jnp.dot(w_c, v_ref[rows, cols], preferred_element_type=F32) * gain + b
            gated_ref[rows, cols] = (u_ref[rows, cols] * mixed).astype(BF16)
    o_ref[...] += jnp.dot(gated_ref[...], wo_ref[...].astype(BF16),
                          preferred_element_type=F32)


def _gm_out(x, u, v, ssq_row, v_gain, w_s, b_s, w_out, *, layer, tm, tc):
    n_tok, d = x.shape
    inner = u.shape[1]
    gw = inner // GM_GROUPS
    gpt = tc // gw
    blocks = [((tm, d), F32), ((tm, tc), F32), ((tm, tc), BF16), ((8, tm), F32), ((8, tc), F32),
              ((gpt, GM_CHUNK, GM_CHUNK), F32), ((gpt, GM_CHUNK, V7X_LANES), F32),
              ((tc, d), F32), ((tm, d), F32)]
    scratch = [((tm, tc), BF16)]
    return pl.pallas_call(
        functools.partial(_gm_out_kernel, inner=inner),
        grid=(n_tok // tm, inner // tc),
        in_specs=[
            pl.BlockSpec((tm, d), lambda i, j: (i, 0)),
            pl.BlockSpec((tm, tc), lambda i, j: (i, j)),
            pl.BlockSpec((tm, tc), lambda i, j: (i, j)),
            pl.BlockSpec((1, tm), lambda i, j: (0, i)),
            pl.BlockSpec((1, tc), lambda i, j: (0, j)),
            pl.BlockSpec((gpt, GM_CHUNK, GM_CHUNK), lambda i, j: (j, 0, 0)),
            pl.BlockSpec((gpt, GM_CHUNK, 1), lambda i, j: (j, 0, 0)),
            pl.BlockSpec((None, tc, d), lambda i, j: (layer, j, 0)),
        ],
        out_specs=pl.BlockSpec((tm, d), lambda i, j: (i, 0)),
        out_shape=jax.ShapeDtypeStruct((n_tok, d), F32),
        scratch_shapes=[pltpu.VMEM(s, t) for s, t in scratch],
        compiler_params=_params(("parallel", "arbitrary"), blocks, scratch,
                                temps=2 * tm * tc * 4 + tc * d * 2),
        name="gm_out",
    )(x, u, v, ssq_row, v_gain, w_s, b_s[..., None], w_out)


def _ssd_in_kernel(x_ref, g_ref, w_ref, wdt_ref, cw_ref, cb_ref, zx_ref, dt_ref,
                   h_ref, pad_ref, carry_ref, *, tiles_per_seq, nz):
    i = pl.program_id(0)
    j = pl.program_id(1)
    tm = zx_ref.shape[0]
    seq_start = (i % tiles_per_seq) == 0

    @pl.when(j == 0)
    def _():
        h_ref[...] = _rmsnorm_rows(x_ref[...], g_ref[...]).astype(BF16)
        dt_ref[...] = lax.dot_general(wdt_ref[...].astype(BF16), h_ref[...], _NT,
                                      preferred_element_type=F32)

    col_tiles, row_tiles = _sub_tiles(w_ref.shape[0], tm)

    def project(rows, w):
        return lax.dot_general(h_ref[rows, :], w, _NT, preferred_element_type=F32)

    @pl.when(j < nz)
    def _():
        for cols in col_tiles:
            w = w_ref[cols, :].astype(BF16)
            for rows in row_tiles:
                acc = project(rows, w)
                zx_ref[rows, cols] = acc * jax.nn.sigmoid(acc)

    @pl.when(j >= nz)
    def _():
        t = j - nz

        @pl.when(seq_start)
        def _():
            pad_ref[0:CONV_HALO, :] = jnp.zeros((CONV_HALO, pad_ref.shape[1]), F32)

        @pl.when(jnp.logical_not(seq_start))
        def _():
            pad_ref[0:CONV_HALO, :] = carry_ref[t]

        for cols in col_tiles:
            w = w_ref[cols, :].astype(BF16)
            for rows in row_tiles:
                acc = project(rows, w)
                n = rows.stop - rows.start
                pad_ref[CONV_HALO + rows.start:CONV_HALO + rows.stop, cols] = acc
                out = cb_ref[:, cols] + cw_ref[SSM_CONV - 1:SSM_CONV, cols] * acc
                for k in range(SSM_CONV - 1):
                    off = rows.start + CONV_HALO - (SSM_CONV - 1) + k
                    out += cw_ref[k:k + 1, cols] * pad_ref[off:off + n, cols]
                zx_ref[rows, cols] = out * jax.nn.sigmoid(out)
            carry_ref[t, :, cols] = pad_ref[tm:tm + CONV_HALO, cols]


def _ssd_in(x, gain, w_in_t, conv_w_t, conv_b, *, layer, n_out, tm, tn, seq):
    n_tok, d = x.shape
    heads = w_in_t.shape[1] - n_out
    nj = n_out // tn
    nz = nj - conv_w_t.shape[1] // tn
    blocks = [((tm, d), F32), ((1, d), F32), ((tn, d), F32), ((heads, d), F32),
              ((8, tn), F32), ((8, tn), F32), ((tm, tn), F32), ((heads, tm), F32)]
    scratch = [((tm, d), BF16), ((tm + CONV_HALO, tn), F32), ((nj - nz, CONV_HALO, tn), F32)]
    return pl.pallas_call(
        functools.partial(_ssd_in_kernel, tiles_per_seq=seq // tm, nz=nz),
        grid=(n_tok // tm, nj),
        in_specs=[
            pl.BlockSpec((tm, d), lambda i, j: (i, 0)),
            pl.BlockSpec((1, d), lambda i, j: (0, 0)),
            pl.BlockSpec((None, tn, d), lambda i, j: (layer, j, 0)),
            pl.BlockSpec((None, heads, d), lambda i, j: (layer, n_out // heads, 0)),
            pl.BlockSpec((SSM_CONV, tn), lambda i, j: (0, jnp.maximum(j - nz, 0))),
            pl.BlockSpec((1, tn), lambda i, j: (0, jnp.maximum(j - nz, 0))),
        ],
        out_specs=[
            pl.BlockSpec((tm, tn), lambda i, j: (i, j)),
            pl.BlockSpec((heads, tm), lambda i, j: (0, i)),
        ],
        out_shape=[jax.ShapeDtypeStruct((n_tok, n_out), F32),
                   jax.ShapeDtypeStruct((heads, n_tok), F32)],
        scratch_shapes=[pltpu.VMEM(s, t) for s, t in scratch],
        compiler_params=_params(("arbitrary", "arbitrary"), blocks, scratch,
                                temps=3 * tm * tn * 4 + d * tn * 2),
        name="ssd_in",
    )(x, gain, w_in_t, w_in_t, conv_w_t, conv_b)


def _ssd_core_kernel(zs_ref, xs_ref, b_ref, c_ref, dtt_ref, dtb_ref, al_ref, dsk_ref, ng_ref,
                     y_ref, state_ref):
    L = SSM_CHUNK
    P = SSM_HEAD_DIM
    N = SSM_STATE
    hpg = SSM_HEADS_PER_GROUP
    gw = SSM_GROUP_WIDTH
    heads = dtt_ref.shape[0]

    @pl.when(pl.program_id(1) == 0)
    def _():
        state_ref[...] = jnp.zeros_like(state_ref)

    dtt = jax.nn.softplus(dtt_ref[...] + dtb_ref[...])
    at = -jnp.exp(al_ref[...])
    r_idx = lax.broadcasted_iota(jnp.int32, (L, L), 0)
    c_idx = lax.broadcasted_iota(jnp.int32, (L, L), 1)
    causal = c_idx <= r_idx
    upper = (r_idx <= c_idx).astype(F32)
    a_cum_t = jnp.dot(dtt * at, upper, preferred_element_type=F32,
                      precision=lax.Precision.HIGHEST)
    last_t = a_cum_t[:, L - 1:L]
    w_t = dtt * jnp.exp(last_t - a_cum_t)
    dk_b = jnp.exp(jnp.broadcast_to(last_t, (heads, N)))
    stack_t = jnp.concatenate(
        [a_cum_t, jnp.zeros((V7X_LANES - heads, L), F32)], axis=0).T
    low_half = lax.broadcasted_iota(jnp.int32, (L, V7X_LANES), 1) < P

    for g in range(SSM_GROUPS):
        gcols = slice(g * gw, (g + 1) * gw)
        bm = b_ref[:, g * N:(g + 1) * N].astype(BF16)
        cm = c_ref[:, g * N:(g + 1) * N].astype(BF16)
        xg = xs_ref[:, gcols]
        st = state_ref[gcols, :]
        hs = range(g * hpg, (g + 1) * hpg)

        cb = lax.dot_general(cm, bm, (((1,), (1,)), ((), ())),
                             preferred_element_type=F32)
        y_off = lax.dot_general(cm, st.astype(BF16), (((1,), (1,)), ((), ())),
                                preferred_element_type=F32)

        w_rows = jnp.concatenate(
            [jnp.broadcast_to(w_t[h:h + 1, :], (P, L)) for h in hs], axis=0)
        dk_rows = jnp.concatenate(
            [jnp.broadcast_to(dk_b[h:h + 1, :], (P, N)) for h in hs], axis=0)
        xw_t = (xg.T * w_rows).astype(BF16)
        state_ref[gcols, :] = st * dk_rows + jnp.dot(xw_t, bm, preferred_element_type=F32)

        y_tiles = []
        for q in range(hpg // HEADS_PER_LANE_TILE):
            qcols = slice(q * V7X_LANES, (q + 1) * V7X_LANES)
            x_q = xg[:, qcols]
            a_cols = []
            scores = []
            for hh in range(HEADS_PER_LANE_TILE):
                h = g * hpg + q * HEADS_PER_LANE_TILE + hh
                a_col = jnp.broadcast_to(stack_t[:, h:h + 1], (L, L))
                seg = a_col - a_cum_t[h:h + 1, :]
                decay = jnp.exp(jnp.where(causal, seg, -jnp.inf))
                scores.append((cb * decay * dtt[h:h + 1, :]).astype(BF16))
                a_cols.append(a_col)
            x_lo = jnp.where(low_half, x_q, 0.0).astype(BF16)
            x_hi = jnp.where(low_half, 0.0, x_q).astype(BF16)
            y_q = jnp.dot(jnp.concatenate(scores, axis=1),
                          jnp.concatenate([x_lo, x_hi], axis=0),
                          preferred_element_type=F32)
            e_q = jnp.exp(jnp.where(low_half, a_cols[0], a_cols[1]))
            y_tiles.append(y_q + y_off[:, qcols] * e_q)

        y = jnp.concatenate(y_tiles, axis=1) + xg * dsk_ref[:, gcols]
        y = y * zs_ref[:, gcols]
        y = y * lax.rsqrt(jnp.mean(y * y, axis=-1, keepdims=True) + EPS)
        y_ref[:, gcols] = (y * ng_ref[:, gcols]).astype(BF16)


def _ssd_core(zx, dtt, dt_bias, a_log, d_skip, norm_g, *, batch, seq):
    n_tok = zx.shape[0]
    inner = norm_g.shape[1]
    heads = dtt.shape[0]
    nc = seq // SSM_CHUNK
    bc_w = SSM_GROUPS * SSM_STATE
    tok = lambda b, c: b * nc + c
    col = lambda v: v.reshape(-1, 1)
    d_lanes = jnp.repeat(d_skip, SSM_HEAD_DIM).reshape(1, inner)
    blocks = [((SSM_CHUNK, inner), F32)] * 2 + [((SSM_CHUNK, bc_w), F32)] * 2 + \
             [((heads, SSM_CHUNK), F32), ((SSM_CHUNK, inner), BF16)]
    scratch = [((inner, SSM_STATE), F32)]
    return pl.pallas_call(
        _ssd_core_kernel,
        grid=(batch, nc),
        in_specs=[
            pl.BlockSpec((SSM_CHUNK, inner), lambda b, c: (tok(b, c), 0)),
            pl.BlockSpec((SSM_CHUNK, inner), lambda b, c: (tok(b, c), 1)),
            pl.BlockSpec((SSM_CHUNK, bc_w), lambda b, c: (tok(b, c), 2 * inner // bc_w)),
            pl.BlockSpec((SSM_CHUNK, bc_w), lambda b, c: (tok(b, c), 2 * inner // bc_w + 1)),
            pl.BlockSpec((heads, SSM_CHUNK), lambda b, c: (0, tok(b, c))),
            pl.BlockSpec((heads, 1), lambda b, c: (0, 0)),
            pl.BlockSpec((heads, 1), lambda b, c: (0, 0)),
            pl.BlockSpec((1, inner), lambda b, c: (0, 0)),
            pl.BlockSpec((1, inner), lambda b, c: (0, 0)),
        ],
        out_specs=pl.BlockSpec((SSM_CHUNK, inner), lambda b, c: (tok(b, c), 0)),
        out_shape=jax.ShapeDtypeStruct((n_tok, inner), BF16),
        scratch_shapes=[pltpu.VMEM(s, t) for s, t in scratch],
        compiler_params=_params(("parallel", "arbitrary"), blocks, scratch, temps=8 << 20),
        name="ssd_core",
    )(zx, zx, zx, zx, dtt, col(dt_bias), col(a_log), d_lanes, norm_g)


def _proj_res_kernel(y_ref, w_ref, x_ref, o_ref, wb_ref):
    @pl.when(pl.program_id(1) == 0)
    def _():
        wb_ref[...] = w_ref[...].astype(BF16)

    o_ref[...] = x_ref[...] + jnp.dot(y_ref[...], wb_ref[...], preferred_element_type=F32)


def _proj_res(y, w, x, *, layer, tm, tn):
    n_tok, k = y.shape
    d = w.shape[2]
    blocks = [((tm, k), BF16), ((k, tn), F32), ((tm, tn), F32), ((tm, tn), F32)]
    scratch = [((k, tn), BF16)]
    return pl.pallas_call(
        _proj_res_kernel,
        grid=(d // tn, n_tok // tm),
        in_specs=[
            pl.BlockSpec((tm, k), lambda j, i: (i, 0)),
            pl.BlockSpec((None, k, tn), lambda j, i: (layer, 0, j)),
            pl.BlockSpec((tm, tn), lambda j, i: (i, j)),
        ],
        out_specs=pl.BlockSpec((tm, tn), lambda j, i: (i, j)),
        out_shape=jax.ShapeDtypeStruct((n_tok, d), F32),
        scratch_shapes=[pltpu.VMEM(s, t) for s, t in scratch],
        compiler_params=_params(("parallel", "arbitrary"), blocks, scratch, temps=tm * tn * 4),
        name="proj_res",
    )(y, w, x)


def _final_norm_kernel(x_ref, g_ref, o_ref):
    o_ref[...] = _rmsnorm_rows(x_ref[...], g_ref[...])


def _final_norm(x, gain, *, tm):
    n_tok, d = x.shape
    blocks = [((tm, d), F32), ((1, d), F32), ((tm, d), F32)]
    return pl.pallas_call(
        _final_norm_kernel,
        grid=(n_tok // tm,),
        in_specs=[pl.BlockSpec((tm, d), lambda i: (i, 0)),
                  pl.BlockSpec((1, d), lambda i: (0, 0))],
        out_specs=pl.BlockSpec((tm, d), lambda i: (i, 0)),
        out_shape=jax.ShapeDtypeStruct((n_tok, d), F32),
        compiler_params=_params(("parallel",), blocks, temps=2 * tm * d * 4),
        name="final_norm",
    )(x, gain)


def kernel(x, ln_ffn_pre, ffn_pre_w_in, ffn_pre_w_out, ln_mix, ln_ffn_post, ffn_post_w_in,
           ffn_post_w_out, gm_w_in, gm_v_norm, gm_w_s, gm_b_s, gm_w_out, ssm_w_in, ssm_conv_w,
           ssm_conv_b, ssm_dt_bias, ssm_a_log, ssm_d, ssm_norm, ssm_w_out, ln_final):
    batch, seq, d = x.shape
    depth = ln_mix.shape[0]
    n_tok = batch * seq
    inner = ssm_norm.shape[1]
    heads = ssm_dt_bias.shape[1]
    zx_cols = ssm_conv_w.shape[1] + inner
    assert seq % SSM_CHUNK == 0 and seq % GM_CHUNK == 0
    assert heads == SSM_GROUPS * SSM_HEADS_PER_GROUP and inner == heads * SSM_HEAD_DIM

    row = lambda v: v.reshape(1, -1)
    ssm_w_in_t = jnp.swapaxes(ssm_w_in, 1, 2)
    xf = x.reshape(n_tok, d)
    for i in range(depth):
        xf = _ffn(xf, row(ln_ffn_pre[i]), ffn_pre_w_in, ffn_pre_w_out, layer=i, tm=1024, tf=512)
        m = i // 2
        if i % 2 == 0:
            u, v, ssq = _gm_in(xf, row(ln_mix[i]), gm_w_in, layer=m, tm=1024, tn=1024)
            xf = _gm_out(xf, u, v, ssq.reshape(1, n_tok), row(gm_v_norm[m]), gm_w_s[m], gm_b_s[m],
                         gm_w_out, layer=m, tm=1024, tc=512)
        else:
            zx, dtt = _ssd_in(xf, row(ln_mix[i]), ssm_w_in_t, ssm_conv_w[m].T,
                              row(ssm_conv_b[m]), layer=m, n_out=zx_cols, tm=1024, tn=1024, seq=seq)
            y = _ssd_core(zx, dtt, ssm_dt_bias[m], ssm_a_log[m], ssm_d[m], row(ssm_norm[m]),
                          batch=batch, seq=seq)
            xf = _proj_res(y, ssm_w_out, xf, layer=m, tm=1024, tn=512)
        xf = _ffn(xf, row(ln_ffn_post[i]), ffn_post_w_in, ffn_post_w_out, layer=i, tm=1024, tf=512)
    out = _final_norm(xf, row(ln_final), tm=512)
    return out.reshape(batch, seq, d)
```

```python
import functools
import math

import jax
import jax.numpy as jnp
from jax import lax
from jax.experimental import pallas as pl
from jax.experimental.pallas import tpu as pltpu

F32 = jnp.float32
BF16 = jnp.bfloat16
EPS = 1e-6

V7X_VMEM_BYTES = 64 * 1024 * 1024
V7X_LANES = 128
V7X_MXU_COLS = 256
MXU_ROW_BLOCK = 512

GM_CHUNK = 128
GM_GROUPS = 16
SSM_HEAD_DIM = 64
SSM_GROUPS = 8
SSM_HEADS_PER_GROUP = 8
SSM_STATE = 128
SSM_CONV = 4
SSM_CHUNK = 128
SSM_GROUP_WIDTH = SSM_HEADS_PER_GROUP * SSM_HEAD_DIM
HEADS_PER_LANE_TILE = V7X_LANES // SSM_HEAD_DIM
CONV_HALO = 8


def _nbytes(shape, dtype):
    return math.prod(shape) * jnp.dtype(dtype).itemsize


def _params(semantics, blocks, scratch=(), temps=0):
    need = 2 * sum(_nbytes(s, d) for s, d in blocks)
    need += sum(_nbytes(s, d) for s, d in scratch) + temps
    limit = min(V7X_VMEM_BYTES - (4 << 20), max(32 << 20, need + (8 << 20)))
    return pltpu.CompilerParams(dimension_semantics=semantics, vmem_limit_bytes=limit)


_NT = (((1,), (1,)), ((), ()))


def _sub_tiles(n_cols, n_rows):
    cols = [slice(c, c + V7X_MXU_COLS) for c in range(0, n_cols, V7X_MXU_COLS)]
    rows = [slice(r, r + MXU_ROW_BLOCK) for r in range(0, n_rows, MXU_ROW_BLOCK)]
    return cols, rows


def _rmsnorm_rows(x, gain):
    ms = jnp.mean(x * x, axis=-1, keepdims=True)
    return x * lax.rsqrt(ms + EPS) * gain


FFN_ROW_BLOCK = 1024


def _ffn_kernel(x_hbm, g_ref, wg_ref, wu_ref, wo_ref, o_hbm, acc_ref, h_ref, in_sem, out_sem):
    i = pl.program_id(0)
    j = pl.program_id(1)
    n_i = pl.num_programs(0)
    n_j = pl.num_programs(1)
    tm = h_ref.shape[0]
    slot = i % 2
    other = 1 - slot

    def x_copy(tile, s):
        return pltpu.make_async_copy(x_hbm.at[pl.ds(tile * tm, tm), :], acc_ref.at[s], in_sem.at[s])

    def out_copy(tile, s):
        return pltpu.make_async_copy(acc_ref.at[s], o_hbm.at[pl.ds(tile * tm, tm), :], out_sem.at[s])

    row_blocks = [slice(r, r + FFN_ROW_BLOCK) for r in range(0, tm, FFN_ROW_BLOCK)]
    acc = acc_ref.at[slot]

    @pl.when((i == 0) & (j == 0))
    def _():
        x_copy(0, 0).start()

    @pl.when(j == 0)
    def _():
        x_copy(i, slot).wait()
        for rows in row_blocks:
            h_ref[rows, :] = _rmsnorm_rows(acc[rows, :], g_ref[...]).astype(BF16)

    @pl.when((j == 1) & (i + 1 < n_i))
    def _():
        @pl.when(i >= 1)
        def _():
            out_copy(i - 1, other).wait()

        x_copy(i + 1, other).start()

    wg = wg_ref[...].astype(BF16)
    wu = wu_ref[...].astype(BF16)
    wo = wo_ref[...].astype(BF16)
    for rows in row_blocks:
        h = h_ref[rows, :]
        gate = jnp.dot(h, wg, preferred_element_type=F32)
        up = jnp.dot(h, wu, preferred_element_type=F32)
        act = (gate * jax.nn.sigmoid(gate) * 0.5 * up).astype(BF16)
        acc[rows, :] += jnp.dot(act, wo, preferred_element_type=F32)

    @pl.when(j == n_j - 1)
    def _():
        out_copy(i, slot).start()

        @pl.when(i == n_i - 1)
        def _():
            @pl.when(n_i >= 2)
            def _():
                out_copy(i - 1, other).wait()

            out_copy(i, slot).wait()


def _ffn(x, gain, w_in, w_out, *, layer, tm, tf):
    n_tok, d = x.shape
    d_ff = w_out.shape[1]
    nf = d_ff // tf
    assert nf >= 2 and tm % FFN_ROW_BLOCK == 0
    blocks = [((1, d), F32), ((d, tf), F32), ((d, tf), F32), ((tf, d), F32)]
    scratch = [((2, tm, d), F32), ((tm, d), BF16)]
    return pl.pallas_call(
        _ffn_kernel,
        grid=(n_tok // tm, nf),
        in_specs=[
            pl.BlockSpec(memory_space=pl.ANY),
            pl.BlockSpec((1, d), lambda i, j: (0, 0)),
            pl.BlockSpec((None, d, tf), lambda i, j: (layer, 0, j)),
            pl.BlockSpec((None, d, tf), lambda i, j: (layer, 0, j + nf)),
            pl.BlockSpec((None, tf, d), lambda i, j: (layer, j, 0)),
        ],
        out_specs=pl.BlockSpec(memory_space=pl.ANY),
        out_shape=jax.ShapeDtypeStruct((n_tok, d), F32),
        scratch_shapes=[pltpu.VMEM(s, t) for s, t in scratch]
        + [pltpu.SemaphoreType.DMA((2,)), pltpu.SemaphoreType.DMA((2,))],
        compiler_params=_params(("arbitrary", "arbitrary"), blocks, scratch,
                                temps=4 * FFN_ROW_BLOCK * tf * 4),
        name="ffn",
    )(x, gain, w_in, w_in, w_out)


def _gm_in_kernel(x_ref, g_ref, w_ref, u_ref, v_ref, ssq_ref, h_ref, *, nu):
    j = pl.program_id(1)

    @pl.when(j == 0)
    def _():
        h_ref[...] = _rmsnorm_rows(x_ref[...], g_ref[...]).astype(BF16)
        ssq_ref[...] = jnp.zeros_like(ssq_ref)

    def gelu_tile(rows, w):
        z = jnp.dot(h_ref[rows, :], w, preferred_element_type=F32)
        return 0.5 * z * (1.0 + lax.erf(z * math.sqrt(0.5)))

    col_tiles, row_tiles = _sub_tiles(w_ref.shape[1], h_ref.shape[0])

    @pl.when(j < nu)
    def _():
        for cols in col_tiles:
            w = w_ref[:, cols].astype(BF16)
            for rows in row_tiles:
                u_ref[rows, cols] = gelu_tile(rows, w)

    @pl.when(j >= nu)
    def _():
        ssq = [jnp.zeros((rows.stop - rows.start, 1), F32) for rows in row_tiles]
        for cols in col_tiles:
            w = w_ref[:, cols].astype(BF16)
            for r, rows in enumerate(row_tiles):
                v = gelu_tile(rows, w)
                v_ref[rows, cols] = v.astype(BF16)
                ssq[r] += jnp.sum(v * v, axis=-1, keepdims=True)
        for r, rows in enumerate(row_tiles):
            ssq_ref[rows, :] += ssq[r]


def _gm_in(x, gain, w_in, *, layer, tm, tn):
    n_tok, d = x.shape
    inner = w_in.shape[2] // 2
    nu = inner // tn
    blocks = [((tm, d), F32), ((1, d), F32), ((d, tn), F32), ((tm, tn), F32), ((tm, tn), BF16),
              ((tm, V7X_LANES), F32)]
    scratch = [((tm, d), BF16)]
    return pl.pallas_call(
        functools.partial(_gm_in_kernel, nu=nu),
        grid=(n_tok // tm, 2 * nu),
        in_specs=[
            pl.BlockSpec((tm, d), lambda i, j: (i, 0)),
            pl.BlockSpec((1, d), lambda i, j: (0, 0)),
            pl.BlockSpec((None, d, tn), lambda i, j: (layer, 0, j)),
        ],
        out_specs=[
            pl.BlockSpec((tm, tn), lambda i, j: (i, jnp.minimum(j, nu - 1))),
            pl.BlockSpec((tm, tn), lambda i, j: (i, jnp.maximum(j - nu, 0))),
            pl.BlockSpec((tm, 1), lambda i, j: (i, 0)),
        ],
        out_shape=[jax.ShapeDtypeStruct((n_tok, inner), F32),
                   jax.ShapeDtypeStruct((n_tok, inner), BF16),
                   jax.ShapeDtypeStruct((n_tok, 1), F32)],
        scratch_shapes=[pltpu.VMEM(s, t) for s, t in scratch],
        compiler_params=_params(("parallel", "arbitrary"), blocks, scratch,
                                temps=4 * tm * tn * 4 + d * tn * 2),
        name="gm_in",
    )(x, gain, w_in)


def _gm_out_kernel(x_ref, u_ref, v_ref, ssq_ref, vg_ref, ws_ref, bs_ref, wo_ref, o_ref,
                   gated_ref, *, inner):
    j = pl.program_id(1)

    @pl.when(j == 0)
    def _():
        o_ref[...] = x_ref[...]

    groups = ws_ref.shape[0]
    gw = v_ref.shape[1] // groups
    tm = v_ref.shape[0]
    t_idx = lax.broadcasted_iota(jnp.int32, (GM_CHUNK, GM_CHUNK), 0)
    s_idx = lax.broadcasted_iota(jnp.int32, (GM_CHUNK, GM_CHUNK), 1)
    causal = s_idx <= t_idx
    scale = lax.rsqrt(ssq_ref[...] / inner + EPS)
    for g in range(groups):
        w = jnp.where(causal, ws_ref[g], 0.0)
        b = bs_ref[g]
        cols = slice(g * gw, (g + 1) * gw)
        gain = vg_ref[:, cols]
        for c in range(tm // GM_CHUNK):
            rows = slice(c * GM_CHUNK, (c + 1) * GM_CHUNK)
            w_c = (w * scale[:, rows]).astype(BF16)
            mixed = jnp.dot(w_c, v_ref[rows, cols], preferred_element_type=F32) * gain + b
            gated_ref[rows, cols] = (u_ref[rows, cols] * mixed).astype(BF16)
    o_ref[...] += jnp.dot(gated_ref[...], wo_ref[...].astype(BF16),
                          preferred_element_type=F32)


def _gm_out(x, u, v, ssq_row, v_gain, w_s, b_s, w_out, *, layer, tm, tc):
    n_tok, d = x.shape
    inner = u.shape[1]
    gw = inner // GM_GROUPS
    gpt = tc // gw
    blocks = [((tm, d), F32), ((tm, tc), F32), ((tm, tc), BF16), ((8, tm), F32), ((8, tc), F32),
              ((gpt, GM_CHUNK, GM_CHUNK), F32), ((gpt, GM_CHUNK, V7X_LANES), F32),
              ((tc, d), F32), ((tm, d), F32)]
    scratch = [((tm, tc), BF16)]
    return pl.pallas_call(
        functools.partial(_gm_out_kernel, inner=inner),
        grid=(n_tok // tm, inner // tc),
        in_specs=[
            pl.BlockSpec((tm, d), lambda i, j: (i, 0)),
            pl.BlockSpec((tm, tc), lambda i, j: (i, j)),
            pl.BlockSpec((tm, tc), lambda i, j: (i, j)),
            pl.BlockSpec((1, tm), lambda i, j: (0, i)),
            pl.BlockSpec((1, tc), lambda i, j: (0, j)),
            pl.BlockSpec((gpt, GM_CHUNK, GM_CHUNK), lambda i, j: (j, 0, 0)),
            pl.BlockSpec((gpt, GM_CHUNK, 1), lambda i, j: (j, 0, 0)),
            pl.BlockSpec((None, tc, d), lambda i, j: (layer, j, 0)),
        ],
        out_specs=pl.BlockSpec((tm, d), lambda i, j: (i, 0)),
        out_shape=jax.ShapeDtypeStruct((n_tok, d), F32),
        scratch_shapes=[pltpu.VMEM(s, t) for s, t in scratch],
        compiler_params=_params(("parallel", "arbitrary"), blocks, scratch,
                                temps=2 * tm * tc * 4 + tc * d * 2),
        name="gm_out",
    )(x, u, v, ssq_row, v_gain, w_s, b_s[..., None], w_out)


def _ssd_in_kernel(x_ref, g_ref, w_ref, wdt_ref, cw_ref, cb_ref, zx_ref, dt_ref,
                   h_ref, pad_ref, carry_ref, *, tiles_per_seq, nz):
    i = pl.program_id(0)
    j = pl.program_id(1)
    tm = zx_ref.shape[0]
    seq_start = (i % tiles_per_seq) == 0

    @pl.when(j == 0)
    def _():
        h_ref[...] = _rmsnorm_rows(x_ref[...], g_ref[...]).astype(BF16)
        dt_ref[...] = lax.dot_general(wdt_ref[...].astype(BF16), h_ref[...], _NT,
                                      preferred_element_type=F32)

    col_tiles, row_tiles = _sub_tiles(w_ref.shape[0], tm)

    def project(rows, w):
        return lax.dot_general(h_ref[rows, :], w, _NT, preferred_element_type=F32)

    @pl.when(j < nz)
    def _():
        for cols in col_tiles:
            w = w_ref[cols, :].astype(BF16)
            for rows in row_tiles:
                acc = project(rows, w)
                zx_ref[rows, cols] = acc * jax.nn.sigmoid(acc)

    @pl.when(j >= nz)
    def _():
        t = j - nz

        @pl.when(seq_start)
        def _():
            pad_ref[0:CONV_HALO, :] = jnp.zeros((CONV_HALO, pad_ref.shape[1]), F32)

        @pl.when(jnp.logical_not(seq_start))
        def _():
            pad_ref[0:CONV_HALO, :] = carry_ref[t]

        for cols in col_tiles:
            w = w_ref[cols, :].astype(BF16)
            for rows in row_tiles:
                acc = project(rows, w)
                n = rows.stop - rows.start
                pad_ref[CONV_HALO + rows.start:CONV_HALO + rows.stop, cols] = acc
                out = cb_ref[:, cols] + cw_ref[SSM_CONV - 1:SSM_CONV, cols] * acc
                for k in range(SSM_CONV - 1):
                    off = rows.start + CONV_HALO - (SSM_CONV - 1) + k
                    out += cw_ref[k:k + 1, cols] * pad_ref[off:off + n, cols]
                zx_ref[rows, cols] = out * jax.nn.sigmoid(out)
            carry_ref[t, :, cols] = pad_ref[tm:tm + CONV_HALO, cols]


def _ssd_in(x, gain, w_in_t, conv_w_t, conv_b, *, layer, n_out, tm, tn, seq):
    n_tok, d = x.shape
    heads = w_in_t.shape[1] - n_out
    nj = n_out // tn
    nz = nj - conv_w_t.shape[1] // tn
    blocks = [((tm, d), F32), ((1, d), F32), ((tn, d), F32), ((heads, d), F32),
              ((8, tn), F32), ((8, tn), F32), ((tm, tn), F32), ((heads, tm), F32)]
    scratch = [((tm, d), BF16), ((tm + CONV_HALO, tn), F32), ((nj - nz, CONV_HALO, tn), F32)]
    return pl.pallas_call(
        functools.partial(_ssd_in_kernel, tiles_per_seq=seq // tm, nz=nz),
        grid=(n_tok // tm, nj),
        in_specs=[
            pl.BlockSpec((tm, d), lambda i, j: (i, 0)),
            pl.BlockSpec((1, d), lambda i, j: (0, 0)),
            pl.BlockSpec((None, tn, d), lambda i, j: (layer, j, 0)),
            pl.BlockSpec((None, heads, d), lambda i, j: (layer, n_out // heads, 0)),
            pl.BlockSpec((SSM_CONV, tn), lambda i, j: (0, jnp.maximum(j - nz, 0))),
            pl.BlockSpec((1, tn), lambda i, j: (0, jnp.maximum(j - nz, 0))),
        ],
        out_specs=[
            pl.BlockSpec((tm, tn), lambda i, j: (i, j)),
            pl.BlockSpec((heads, tm), lambda i, j: (0, i)),
        ],
        out_shape=[jax.ShapeDtypeStruct((n_tok, n_out), F32),
                   jax.ShapeDtypeStruct((heads, n_tok), F32)],
        scratch_shapes=[pltpu.VMEM(s, t) for s, t in scratch],
        compiler_params=_params(("arbitrary", "arbitrary"), blocks, scratch,
                                temps=3 * tm * tn * 4 + d * tn * 2),
        name="ssd_in",
    )(x, gain, w_in_t, w_in_t, conv_w_t, conv_b)


def _ssd_core_kernel(zs_ref, xs_ref, b_ref, c_ref, dtt_ref, dtb_ref, al_ref, dsk_ref, ng_ref,
                     y_ref, state_ref):
    L = SSM_CHUNK
    P = SSM_HEAD_DIM
    N = SSM_STATE
    hpg = SSM_HEADS_PER_GROUP
    gw = SSM_GROUP_WIDTH
    heads = dtt_ref.shape[0]

    @pl.when(pl.program_id(1) == 0)
    def _():
        state_ref[...] = jnp.zeros_like(state_ref)

    dtt = jax.nn.softplus(dtt_ref[...] + dtb_ref[...])
    at = -jnp.exp(al_ref[...])
    r_idx = lax.broadcasted_iota(jnp.int32, (L, L), 0)
    c_idx = lax.broadcasted_iota(jnp.int32, (L, L), 1)
    causal = c_idx <= r_idx
    upper = (r_idx <= c_idx).astype(F32)
    a_cum_t = jnp.dot(dtt * at, upper, preferred_element_type=F32,
                      precision=lax.Precision.HIGHEST)
    last_t = a_cum_t[:, L - 1:L]
    w_t = dtt * jnp.exp(last_t - a_cum_t)
    dk_b = jnp.exp(jnp.broadcast_to(last_t, (heads, N)))
    stack_t = jnp.concatenate(
        [a_cum_t, jnp.zeros((V7X_LANES - heads, L), F32)], axis=0).T
    low_half = lax.broadcasted_iota(jnp.int32, (L, V7X_LANES), 1) < P

    for g in range(SSM_GROUPS):
        gcols = slice(g * gw, (g + 1) * gw)
        bm = b_ref[:, g * N:(g + 1) * N].astype(BF16)
        cm = c_ref[:, g * N:(g + 1) * N].astype(BF16)
        xg = xs_ref[:, gcols]
        st = state_ref[gcols, :]
        hs = range(g * hpg, (g + 1) * hpg)

        cb = lax.dot_general(cm, bm, (((1,), (1,)), ((), ())),
                             preferred_element_type=F32)
        y_off = lax.dot_general(cm, st.astype(BF16), (((1,), (1,)), ((), ())),
                                preferred_element_type=F32)

        w_rows = jnp.concatenate(
            [jnp.broadcast_to(w_t[h:h + 1, :], (P, L)) for h in hs], axis=0)
        dk_rows = jnp.concatenate(
            [jnp.broadcast_to(dk_b[h:h + 1, :], (P, N)) for h in hs], axis=0)
        xw_t = (xg.T * w_rows).astype(BF16)
        state_ref[gcols, :] = st * dk_rows + jnp.dot(xw_t, bm, preferred_element_type=F32)

        y_tiles = []
        for q in range(hpg // HEADS_PER_LANE_TILE):
            qcols = slice(q * V7X_LANES, (q + 1) * V7X_LANES)
            x_q = xg[:, qcols]
            a_cols = []
            scores = []
            for hh in range(HEADS_PER_LANE_TILE):
                h = g * hpg + q * HEADS_PER_LANE_TILE + hh
                a_col = jnp.broadcast_to(stack_t[:, h:h + 1], (L, L))
                seg = a_col - a_cum_t[h:h + 1, :]
                decay = jnp.exp(jnp.where(causal, seg, -jnp.inf))
                scores.append((cb * decay * dtt[h:h + 1, :]).astype(BF16))
                a_cols.append(a_col)
            x_lo = jnp.where(low_half, x_q, 0.0).astype(BF16)
            x_hi = jnp.where(low_half, 0.0, x_q).astype(BF16)
            y_q = jnp.dot(jnp.concatenate(scores, axis=1),
                          jnp.concatenate([x_lo, x_hi], axis=0),
                          preferred_element_type=F32)
            e_q = jnp.exp(jnp.where(low_half, a_cols[0], a_cols[1]))
            y_tiles.append(y_q + y_off[:, qcols] * e_q)

        y = jnp.concatenate(y_tiles, axis=1) + xg * dsk_ref[:, gcols]
        y = y * zs_ref[:, gcols]
        y = y * lax.rsqrt(jnp.mean(y * y, axis=-1, keepdims=True) + EPS)
        y_ref[:, gcols] = (y * ng_ref[:, gcols]).astype(BF16)


def _ssd_core(zx, dtt, dt_bias, a_log, d_skip, norm_g, *, batch, seq):
    n_tok = zx.shape[0]
    inner = norm_g.shape[1]
    heads = dtt.shape[0]
    nc = seq // SSM_CHUNK
    bc_w = SSM_GROUPS * SSM_STATE
    tok = lambda b, c: b * nc + c
    col = lambda v: v.reshape(-1, 1)
    d_lanes = jnp.repeat(d_skip, SSM_HEAD_DIM).reshape(1, inner)
    blocks = [((SSM_CHUNK, inner), F32)] * 2 + [((SSM_CHUNK, bc_w), F32)] * 2 + \
             [((heads, SSM_CHUNK), F32), ((SSM_CHUNK, inner), BF16)]
    scratch = [((inner, SSM_STATE), F32)]
    return pl.pallas_call(
        _ssd_core_kernel,
        grid=(batch, nc),
        in_specs=[
            pl.BlockSpec((SSM_CHUNK, inner), lambda b, c: (tok(b, c), 0)),
            pl.BlockSpec((SSM_CHUNK, inner), lambda b, c: (tok(b, c), 1)),
            pl.BlockSpec((SSM_CHUNK, bc_w), lambda b, c: (tok(b, c), 2 * inner // bc_w)),
            pl.BlockSpec((SSM_CHUNK, bc_w), lambda b, c: (tok(b, c), 2 * inner // bc_w + 1)),
            pl.BlockSpec((heads, SSM_CHUNK), lambda b, c: (0, tok(b, c))),
            pl.BlockSpec((heads, 1), lambda b, c: (0, 0)),
            pl.BlockSpec((heads, 1), lambda b, c: (0, 0)),
            pl.BlockSpec((1, inner), lambda b, c: (0, 0)),
            pl.BlockSpec((1, inner), lambda b, c: (0, 0)),
        ],
        out_specs=pl.BlockSpec((SSM_CHUNK, inner), lambda b, c: (tok(b, c), 0)),
        out_shape=jax.ShapeDtypeStruct((n_tok, inner), BF16),
        scratch_shapes=[pltpu.VMEM(s, t) for s, t in scratch],
        compiler_params=_params(("parallel", "arbitrary"), blocks, scratch, temps=8 << 20),
        name="ssd_core",
    )(zx, zx, zx, zx, dtt, col(dt_bias), col(a_log), d_lanes, norm_g)


def _proj_res_kernel(y_ref, w_ref, x_ref, o_ref, wb_ref):
    @pl.when(pl.program_id(1) == 0)
    def _():
        wb_ref[...] = w_ref[...].astype(BF16)

    o_ref[...] = x_ref[...] + jnp.dot(y_ref[...], wb_ref[...], preferred_element_type=F32)


def _proj_res(y, w, x, *, layer, tm, tn):
    n_tok, k = y.shape
    d = w.shape[2]
    blocks = [((tm, k), BF16), ((k, tn), F32), ((tm, tn), F32), ((tm, tn), F32)]
    scratch = [((k, tn), BF16)]
    return pl.pallas_call(
        _proj_res_kernel,
        grid=(d // tn, n_tok // tm),
        in_specs=[
            pl.BlockSpec((tm, k), lambda j, i: (i, 0)),
            pl.BlockSpec((None, k, tn), lambda j, i: (layer, 0, j)),
            pl.BlockSpec((tm, tn), lambda j, i: (i, j)),
        ],
        out_specs=pl.BlockSpec((tm, tn), lambda j, i: (i, j)),
        out_shape=jax.ShapeDtypeStruct((n_tok, d), F32),
        scratch_shapes=[pltpu.VMEM(s, t) for s, t in scratch],
        compiler_params=_params(("parallel", "arbitrary"), blocks, scratch, temps=tm * tn * 4),
        name="proj_res",
    )(y, w, x)


def _final_norm_kernel(x_ref, g_ref, o_ref):
    o_ref[...] = _rmsnorm_rows(x_ref[...], g_ref[...])


def _final_norm(x, gain, *, tm):
    n_tok, d = x.shape
    blocks = [((tm, d), F32), ((1, d), F32), ((tm, d), F32)]
    return pl.pallas_call(
        _final_norm_kernel,
        grid=(n_tok // tm,),
        in_specs=[pl.BlockSpec((tm, d), lambda i: (i, 0)),
                  pl.BlockSpec((1, d), lambda i: (0, 0))],
        out_specs=pl.BlockSpec((tm, d), lambda i: (i, 0)),
        out_shape=jax.ShapeDtypeStruct((n_tok, d), F32),
        compiler_params=_params(("parallel",), blocks, temps=2 * tm * d * 4),
        name="final_norm",
    )(x, gain)


def kernel(x, ln_ffn_pre, ffn_pre_w_in, ffn_pre_w_out, ln_mix, ln_ffn_post, ffn_post_w_in,
           ffn_post_w_out, gm_w_in, gm_v_norm, gm_w_s, gm_b_s, gm_w_out, ssm_w_in, ssm_conv_w,
           ssm_conv_b, ssm_dt_bias, ssm_a_log, ssm_d, ssm_norm, ssm_w_out, ln_final):
    batch, seq, d = x.shape
    depth = ln_mix.shape[0]
    n_tok = batch * seq
    inner = ssm_norm.shape[1]
    heads = ssm_dt_bias.shape[1]
    zx_cols = ssm_conv_w.shape[1] + inner
    assert seq % SSM_CHUNK == 0 and seq % GM_CHUNK == 0
    assert heads == SSM_GROUPS * SSM_HEADS_PER_GROUP and inner == heads * SSM_HEAD_DIM

    row = lambda v: v.reshape(1, -1)
    ssm_w_in_t = jnp.swapaxes(ssm_w_in, 1, 2)
    xf = x.reshape(n_tok, d)
    for i in range(depth):
        xf = _ffn(xf, row(ln_ffn_pre[i]), ffn_pre_w_in, ffn_pre_w_out, layer=i, tm=2048, tf=256)
        m = i // 2
        if i % 2 == 0:
            u, v, ssq = _gm_in(xf, row(ln_mix[i]), gm_w_in, layer=m, tm=1024, tn=1024)
            xf = _gm_out(xf, u, v, ssq.reshape(1, n_tok), row(gm_v_norm[m]), gm_w_s[m], gm_b_s[m],
                         gm_w_out, layer=m, tm=1024, tc=512)
        else:
            zx, dtt = _ssd_in(xf, row(ln_mix[i]), ssm_w_in_t, ssm_conv_w[m].T,
                              row(ssm_conv_b[m]), layer=m, n_out=zx_cols, tm=1024, tn=1024, seq=seq)
            y = _ssd_core(zx, dtt, ssm_dt_bias[m], ssm_a_log[m], ssm_d[m], row(ssm_norm[m]),
                          batch=batch, seq=seq)
            xf = _proj_res(y, ssm_w_out, xf, layer=m, tm=1024, tn=512)
        xf = _ffn(xf, row(ln_ffn_post[i]), ffn_post_w_in, ffn_post_w_out, layer=i, tm=2048, tf=256)
    out = _final_norm(xf, row(ln_final), tm=512)
    return out.reshape(batch, seq, d)
```

```python
import functools
import math

import jax
import jax.numpy as jnp
from jax import lax
from jax.experimental import pallas as pl
from jax.experimental.pallas import tpu as pltpu

F32 = jnp.float32
BF16 = jnp.bfloat16
EPS = 1e-6

V7X_VMEM_BYTES = 64 * 1024 * 1024
V7X_LANES = 128
V7X_MXU_COLS = 256
MXU_ROW_BLOCK = 512

GM_CHUNK = 128
GM_GROUPS = 16
SSM_HEAD_DIM = 64
SSM_GROUPS = 8
SSM_HEADS_PER_GROUP = 8
SSM_STATE = 128
SSM_CONV = 4
SSM_CHUNK = 128
SSM_GROUP_WIDTH = SSM_HEADS_PER_GROUP * SSM_HEAD_DIM
HEADS_PER_LANE_TILE = V7X_LANES // SSM_HEAD_DIM
CONV_HALO = 8


def _nbytes(shape, dtype):
    return math.prod(shape) * jnp.dtype(dtype).itemsize


def _params(semantics, blocks, scratch=(), temps=0, flags=None):
    need = 2 * sum(_nbytes(s, d) for s, d in blocks)
    need += sum(_nbytes(s, d) for s, d in scratch) + temps
    limit = min(V7X_VMEM_BYTES - (4 << 20), max(32 << 20, need + (8 << 20)))
    return pltpu.CompilerParams(dimension_semantics=semantics, vmem_limit_bytes=limit,
                                flags=flags)


_NT = (((1,), (1,)), ((), ()))


def _sub_tiles(n_cols, n_rows):
    cols = [slice(c, c + V7X_MXU_COLS) for c in range(0, n_cols, V7X_MXU_COLS)]
    rows = [slice(r, r + MXU_ROW_BLOCK) for r in range(0, n_rows, MXU_ROW_BLOCK)]
    return cols, rows


def _rmsnorm_rows(x, gain):
    ms = jnp.mean(x * x, axis=-1, keepdims=True)
    return x * lax.rsqrt(ms + EPS) * gain


ACC_ROW_BLOCK = 1024


class _ResidualSlots:
    def __init__(self, x_hbm, o_hbm, acc_ref, in_sem, out_sem):
        self.i = pl.program_id(0)
        self.j = pl.program_id(1)
        self.n_i = pl.num_programs(0)
        self.n_j = pl.num_programs(1)
        self.tm = acc_ref.shape[1]
        self.slot = self.i % 2
        self.other = 1 - self.slot
        self.acc = acc_ref.at[self.slot]
        self._refs = (x_hbm, o_hbm, acc_ref, in_sem, out_sem)

    def _x_copy(self, tile, s):
        x_hbm, _, acc_ref, in_sem, _ = self._refs
        return pltpu.make_async_copy(x_hbm.at[pl.ds(tile * self.tm, self.tm), :],
                                     acc_ref.at[s], in_sem.at[s])

    def _out_copy(self, tile, s):
        _, o_hbm, acc_ref, _, out_sem = self._refs
        return pltpu.make_async_copy(acc_ref.at[s],
                                     o_hbm.at[pl.ds(tile * self.tm, self.tm), :], out_sem.at[s])

    def begin_step(self, on_tile_loaded=None):
        i, j = self.i, self.j

        @pl.when((i == 0) & (j == 0))
        def _():
            self._x_copy(0, 0).start()

        @pl.when(j == 0)
        def _():
            self._x_copy(i, self.slot).wait()
            if on_tile_loaded is not None:
                on_tile_loaded()

        @pl.when((j == 1) & (i + 1 < self.n_i))
        def _():
            @pl.when(i >= 1)
            def _():
                self._out_copy(i - 1, self.other).wait()

            self._x_copy(i + 1, self.other).start()

    def end_step(self, before_write_back=None):
        i = self.i

        @pl.when(self.j == self.n_j - 1)
        def _():
            if before_write_back is not None:
                before_write_back()
            self._out_copy(i, self.slot).start()

            @pl.when(i == self.n_i - 1)
            def _():
                @pl.when(self.n_i >= 2)
                def _():
                    self._out_copy(i - 1, self.other).wait()

                self._out_copy(i, self.slot).wait()


def _ffn_kernel(x_hbm, g_ref, fg_ref, wg_ref, wu_ref, wo_ref, o_hbm, acc_ref, h_ref,
                in_sem, out_sem, *, final_norm):
    slots = _ResidualSlots(x_hbm, o_hbm, acc_ref, in_sem, out_sem)
    acc = slots.acc
    row_blocks = [slice(r, r + ACC_ROW_BLOCK) for r in range(0, slots.tm, ACC_ROW_BLOCK)]

    def norm_input():
        for rows in row_blocks:
            h_ref[rows, :] = _rmsnorm_rows(acc[rows, :], g_ref[...]).astype(BF16)

    slots.begin_step(on_tile_loaded=norm_input)

    wg = wg_ref[...].astype(BF16)
    wu = wu_ref[...].astype(BF16)
    wo = wo_ref[...].astype(BF16)
    for rows in row_blocks:
        h = h_ref[rows, :]
        gate = jnp.dot(h, wg, preferred_element_type=F32)
        up = jnp.dot(h, wu, preferred_element_type=F32)
        act = (gate * jax.nn.sigmoid(gate) * 0.5 * up).astype(BF16)
        acc[rows, :] += jnp.dot(act, wo, preferred_element_type=F32)

    def norm_output():
        for rows in row_blocks:
            acc[rows, :] = _rmsnorm_rows(acc[rows, :], fg_ref[...])

    slots.end_step(before_write_back=norm_output if final_norm else None)


def _ffn(x, gain, w_in, w_out, final_gain, *, layer, tm, tf, final_norm):
    n_tok, d = x.shape
    d_ff = w_out.shape[1]
    nf = d_ff // tf
    assert nf >= 2 and tm % ACC_ROW_BLOCK == 0
    blocks = [((1, d), F32), ((1, d), F32), ((d, tf), F32), ((d, tf), F32), ((tf, d), F32)]
    scratch = [((2, tm, d), F32), ((tm, d), BF16)]
    return pl.pallas_call(
        functools.partial(_ffn_kernel, final_norm=final_norm),
        grid=(n_tok // tm, nf),
        in_specs=[
            pl.BlockSpec(memory_space=pl.ANY),
            pl.BlockSpec((1, d), lambda i, j: (0, 0)),
            pl.BlockSpec((1, d), lambda i, j: (0, 0)),
            pl.BlockSpec((None, d, tf), lambda i, j: (layer, 0, j)),
            pl.BlockSpec((None, d, tf), lambda i, j: (layer, 0, j + nf)),
            pl.BlockSpec((None, tf, d), lambda i, j: (layer, j, 0)),
        ],
        out_specs=pl.BlockSpec(memory_space=pl.ANY),
        out_shape=jax.ShapeDtypeStruct((n_tok, d), F32),
        scratch_shapes=[pltpu.VMEM(s, t) for s, t in scratch]
        + [pltpu.SemaphoreType.DMA((2,)), pltpu.SemaphoreType.DMA((2,))],
        compiler_params=_params(("arbitrary", "arbitrary"), blocks, scratch,
                                temps=4 * ACC_ROW_BLOCK * tf * 4),
        name="ffn",
    )(x, gain, final_gain, w_in, w_in, w_out)


def _gm_in_kernel(x_ref, g_ref, w_ref, u_ref, v_ref, ssq_ref, h_ref, *, nu):
    j = pl.program_id(1)

    @pl.when(j == 0)
    def _():
        h_ref[...] = _rmsnorm_rows(x_ref[...], g_ref[...]).astype(BF16)
        ssq_ref[...] = jnp.zeros_like(ssq_ref)

    def gelu_tile(rows, w):
        z = jnp.dot(h_ref[rows, :], w, preferred_element_type=F32)
        return 0.5 * z * (1.0 + lax.erf(z * math.sqrt(0.5)))

    col_tiles, row_tiles = _sub_tiles(w_ref.shape[1], h_ref.shape[0])

    @pl.when(j < nu)
    def _():
        for cols in col_tiles:
            w = w_ref[:, cols].astype(BF16)
            for rows in row_tiles:
                u_ref[rows, cols] = gelu_tile(rows, w)

    @pl.when(j >= nu)
    def _():
        ssq = [jnp.zeros((rows.stop - rows.start, 1), F32) for rows in row_tiles]
        for cols in col_tiles:
            w = w_ref[:, cols].astype(BF16)
            for r, rows in enumerate(row_tiles):
                v = gelu_tile(rows, w)
                v_ref[rows, cols] = v.astype(BF16)
                ssq[r] += jnp.sum(v * v, axis=-1, keepdims=True)
        for r, rows in enumerate(row_tiles):
            ssq_ref[rows, :] += ssq[r]


def _gm_in(x, gain, w_in, *, layer, tm, tn):
    n_tok, d = x.shape
    inner = w_in.shape[2] // 2
    nu = inner // tn
    blocks = [((tm, d), F32), ((1, d), F32), ((d, tn), F32), ((tm, tn), F32), ((tm, tn), BF16),
              ((tm, V7X_LANES), F32)]
    scratch = [((tm, d), BF16)]
    return pl.pallas_call(
        functools.partial(_gm_in_kernel, nu=nu),
        grid=(n_tok // tm, 2 * nu),
        in_specs=[
            pl.BlockSpec((tm, d), lambda i, j: (i, 0)),
            pl.BlockSpec((1, d), lambda i, j: (0, 0)),
            pl.BlockSpec((None, d, tn), lambda i, j: (layer, 0, j)),
        ],
        out_specs=[
            pl.BlockSpec((tm, tn), lambda i, j: (i, jnp.minimum(j, nu - 1))),
            pl.BlockSpec((tm, tn), lambda i, j: (i, jnp.maximum(j - nu, 0))),
            pl.BlockSpec((tm, 1), lambda i, j: (i, 0)),
        ],
        out_shape=[jax.ShapeDtypeStruct((n_tok, inner), F32),
                   jax.ShapeDtypeStruct((n_tok, inner), BF16),
                   jax.ShapeDtypeStruct((n_tok, 1), F32)],
        scratch_shapes=[pltpu.VMEM(s, t) for s, t in scratch],
        compiler_params=_params(("parallel", "arbitrary"), blocks, scratch,
                                temps=4 * tm * tn * 4 + d * tn * 2),
        name="gm_in",
    )(x, gain, w_in)


def _gm_out_kernel(x_hbm, u_ref, v_ref, ssq_ref, vg_ref, ws_ref, bs_ref, wo_ref, o_hbm,
                   acc_ref, gated_ref, in_sem, out_sem, *, inner):
    slots = _ResidualSlots(x_hbm, o_hbm, acc_ref, in_sem, out_sem)
    slots.begin_step()
    acc = slots.acc

    groups = ws_ref.shape[0]
    gw = v_ref.shape[1] // groups
    tm = v_ref.shape[0]
    t_idx = lax.broadcasted_iota(jnp.int32, (GM_CHUNK, GM_CHUNK), 0)
    s_idx = lax.broadcasted_iota(jnp.int32, (GM_CHUNK, GM_CHUNK), 1)
    causal = s_idx <= t_idx
    scale = lax.rsqrt(ssq_ref[...] / inner + EPS)
    for g in range(groups):
        w = jnp.where(causal, ws_ref[g], 0.0)
        b = bs_ref[g]
        cols = slice(g * gw, (g + 1) * gw)
        gain = vg_ref[:, cols]
        for c in range(tm // GM_CHUNK):
            rows = slice(c * GM_CHUNK, (c + 1) * GM_CHUNK)
            w_c = (w * scale[:, rows]).astype(BF16)
            mixed = jnp.dot(w_c, v_ref[rows, cols], preferred_element_type=F32) * gain + b
            gated_ref[rows, cols] = (u_ref[rows, cols] * mixed).astype(BF16)
    wo = wo_ref[...].astype(BF16)
    for r in range(0, tm, ACC_ROW_BLOCK):
        rows = slice(r, r + ACC_ROW_BLOCK)
        acc[rows, :] += jnp.dot(gated_ref[rows, :], wo, preferred_element_type=F32)
    slots.end_step()


def _gm_out(x, u, v, ssq_row, v_gain, w_s, b_s, w_out, *, layer, tm, tc):
    n_tok, d = x.shape
    inner = u.shape[1]
    gw = inner // GM_GROUPS
    gpt = tc // gw
    assert inner // tc >= 2 and tm % ACC_ROW_BLOCK == 0
    blocks = [((tm, tc), F32), ((tm, tc), BF16), ((8, tm), F32), ((8, tc), F32),
              ((gpt, GM_CHUNK, GM_CHUNK), F32), ((gpt, GM_CHUNK, V7X_LANES), F32),
              ((tc, d), F32)]
    scratch = [((2, tm, d), F32), ((tm, tc), BF16)]
    return pl.pallas_call(
        functools.partial(_gm_out_kernel, inner=inner),
        grid=(n_tok // tm, inner // tc),
        in_specs=[
            pl.BlockSpec(memory_space=pl.ANY),
            pl.BlockSpec((tm, tc), lambda i, j: (i, j)),
            pl.BlockSpec((tm, tc), lambda i, j: (i, j)),
            pl.BlockSpec((1, tm), lambda i, j: (0, i)),
            pl.BlockSpec((1, tc), lambda i, j: (0, j)),
            pl.BlockSpec((gpt, GM_CHUNK, GM_CHUNK), lambda i, j: (j, 0, 0)),
            pl.BlockSpec((gpt, GM_CHUNK, 1), lambda i, j: (j, 0, 0)),
            pl.BlockSpec((None, tc, d), lambda i, j: (layer, j, 0)),
        ],
        out_specs=pl.BlockSpec(memory_space=pl.ANY),
        out_shape=jax.ShapeDtypeStruct((n_tok, d), F32),
        scratch_shapes=[pltpu.VMEM(s, t) for s, t in scratch]
        + [pltpu.SemaphoreType.DMA((2,)), pltpu.SemaphoreType.DMA((2,))],
        compiler_params=_params(("arbitrary", "arbitrary"), blocks, scratch,
                                temps=2 * ACC_ROW_BLOCK * tc * 4 + tc * d * 2),
        name="gm_out",
    )(x, u, v, ssq_row, v_gain, w_s, b_s[..., None], w_out)


def _ssd_in_kernel(x_ref, g_ref, w_ref, wdt_ref, cw_ref, cb_ref, zx_ref, dt_ref,
                   h_ref, pad_ref, carry_ref, *, tiles_per_seq, nz):
    i = pl.program_id(0)
    j = pl.program_id(1)
    tm = zx_ref.shape[0]
    seq_start = (i % tiles_per_seq) == 0

    @pl.when(j == 0)
    def _():
        h_ref[...] = _rmsnorm_rows(x_ref[...], g_ref[...]).astype(BF16)
        dt_ref[...] = lax.dot_general(wdt_ref[...].astype(BF16), h_ref[...], _NT,
                                      preferred_element_type=F32)

    col_tiles, row_tiles = _sub_tiles(w_ref.shape[0], tm)

    def project(rows, w):
        return lax.dot_general(h_ref[rows, :], w, _NT, preferred_element_type=F32)

    @pl.when(j < nz)
    def _():
        for cols in col_tiles:
            w = w_ref[cols, :].astype(BF16)
            for rows in row_tiles:
                acc = project(rows, w)
                zx_ref[rows, cols] = acc * jax.nn.sigmoid(acc)

    @pl.when(j >= nz)
    def _():
        t = j - nz

        @pl.when(seq_start)
        def _():
            pad_ref[0:CONV_HALO, :] = jnp.zeros((CONV_HALO, pad_ref.shape[1]), F32)

        @pl.when(jnp.logical_not(seq_start))
        def _():
            pad_ref[0:CONV_HALO, :] = carry_ref[t]

        for cols in col_tiles:
            w = w_ref[cols, :].astype(BF16)
            for rows in row_tiles:
                acc = project(rows, w)
                n = rows.stop - rows.start
                pad_ref[CONV_HALO + rows.start:CONV_HALO + rows.stop, cols] = acc
                out = cb_ref[:, cols] + cw_ref[SSM_CONV - 1:SSM_CONV, cols] * acc
                for k in range(SSM_CONV - 1):
                    off = rows.start + CONV_HALO - (SSM_CONV - 1) + k
                    out += cw_ref[k:k + 1, cols] * pad_ref[off:off + n, cols]
                zx_ref[rows, cols] = out * jax.nn.sigmoid(out)
            carry_ref[t, :, cols] = pad_ref[tm:tm + CONV_HALO, cols]


def _ssd_in(x, gain, w_in_t, conv_w_t, conv_b, *, layer, n_out, tm, tn, seq):
    n_tok, d = x.shape
    heads = w_in_t.shape[1] - n_out
    nj = n_out // tn
    nz = nj - conv_w_t.shape[1] // tn
    blocks = [((tm, d), F32), ((1, d), F32), ((tn, d), F32), ((heads, d), F32),
              ((8, tn), F32), ((8, tn), F32), ((tm, tn), F32), ((heads, tm), F32)]
    scratch = [((tm, d), BF16), ((tm + CONV_HALO, tn), F32), ((nj - nz, CONV_HALO, tn), F32)]
    return pl.pallas_call(
        functools.partial(_ssd_in_kernel, tiles_per_seq=seq // tm, nz=nz),
        grid=(n_tok // tm, nj),
        in_specs=[
            pl.BlockSpec((tm, d), lambda i, j: (i, 0)),
            pl.BlockSpec((1, d), lambda i, j: (0, 0)),
            pl.BlockSpec((None, tn, d), lambda i, j: (layer, j, 0)),
            pl.BlockSpec((None, heads, d), lambda i, j: (layer, n_out // heads, 0)),
            pl.BlockSpec((SSM_CONV, tn), lambda i, j: (0, jnp.maximum(j - nz, 0))),
            pl.BlockSpec((1, tn), lambda i, j: (0, jnp.maximum(j - nz, 0))),
        ],
        out_specs=[
            pl.BlockSpec((tm, tn), lambda i, j: (i, j)),
            pl.BlockSpec((heads, tm), lambda i, j: (0, i)),
        ],
        out_shape=[jax.ShapeDtypeStruct((n_tok, n_out), F32),
                   jax.ShapeDtypeStruct((heads, n_tok), F32)],
        scratch_shapes=[pltpu.VMEM(s, t) for s, t in scratch],
        compiler_params=_params(("arbitrary", "arbitrary"), blocks, scratch,
                                temps=3 * tm * tn * 4 + d * tn * 2),
        name="ssd_in",
    )(x, gain, w_in_t, w_in_t, conv_w_t, conv_b)


def _ssd_core_kernel(zs_ref, xs_ref, b_ref, c_ref, dtt_ref, dtb_ref, al_ref, dsk_ref, ng_ref,
                     y_ref, state_ref):
    L = SSM_CHUNK
    P = SSM_HEAD_DIM
    N = SSM_STATE
    hpg = SSM_HEADS_PER_GROUP
    gw = SSM_GROUP_WIDTH
    heads = dtt_ref.shape[0]

    @pl.when(pl.program_id(1) == 0)
    def _():
        state_ref[...] = jnp.zeros_like(state_ref)

    dtt = jax.nn.softplus(dtt_ref[...] + dtb_ref[...])
    at = -jnp.exp(al_ref[...])
    r_idx = lax.broadcasted_iota(jnp.int32, (L, L), 0)
    c_idx = lax.broadcasted_iota(jnp.int32, (L, L), 1)
    causal = c_idx <= r_idx
    upper = (r_idx <= c_idx).astype(F32)
    a_cum_t = jnp.dot(dtt * at, upper, preferred_element_type=F32,
                      precision=lax.Precision.HIGHEST)
    last_t = a_cum_t[:, L - 1:L]
    w_t = dtt * jnp.exp(last_t - a_cum_t)
    dk_b = jnp.exp(jnp.broadcast_to(last_t, (heads, N)))
    stack_t = jnp.concatenate(
        [a_cum_t, jnp.zeros((V7X_LANES - heads, L), F32)], axis=0).T
    low_half = lax.broadcasted_iota(jnp.int32, (L, V7X_LANES), 1) < P

    for g in range(SSM_GROUPS):
        gcols = slice(g * gw, (g + 1) * gw)
        bm = b_ref[:, g * N:(g + 1) * N].astype(BF16)
        cm = c_ref[:, g * N:(g + 1) * N].astype(BF16)
        xg = xs_ref[:, gcols]
        st = state_ref[gcols, :]
        hs = range(g * hpg, (g + 1) * hpg)

        cb = lax.dot_general(cm, bm, (((1,), (1,)), ((), ())),
                             preferred_element_type=F32)
        y_off = lax.dot_general(cm, st.astype(BF16), (((1,), (1,)), ((), ())),
                                preferred_element_type=F32)

        w_rows = jnp.concatenate(
            [jnp.broadcast_to(w_t[h:h + 1, :], (P, L)) for h in hs], axis=0)
        dk_rows = jnp.concatenate(
            [jnp.broadcast_to(dk_b[h:h + 1, :], (P, N)) for h in hs], axis=0)
        xw_t = (xg.T * w_rows).astype(BF16)
        state_ref[gcols, :] = st * dk_rows + jnp.dot(xw_t, bm, preferred_element_type=F32)

        y_tiles = []
        for q in range(hpg // HEADS_PER_LANE_TILE):
            qcols = slice(q * V7X_LANES, (q + 1) * V7X_LANES)
            x_q = xg[:, qcols]
            a_cols = []
            scores = []
            for hh in range(HEADS_PER_LANE_TILE):
                h = g * hpg + q * HEADS_PER_LANE_TILE + hh
                a_col = jnp.broadcast_to(stack_t[:, h:h + 1], (L, L))
                seg = a_col - a_cum_t[h:h + 1, :]
                decay = jnp.exp(jnp.where(causal, seg, -jnp.inf))
                scores.append((cb * decay * dtt[h:h + 1, :]).astype(BF16))
                a_cols.append(a_col)
            x_lo = jnp.where(low_half, x_q, 0.0).astype(BF16)
            x_hi = jnp.where(low_half, 0.0, x_q).astype(BF16)
            y_q = jnp.dot(jnp.concatenate(scores, axis=1),
                          jnp.concatenate([x_lo, x_hi], axis=0),
                          preferred_element_type=F32)
            e_q = jnp.exp(jnp.where(low_half, a_cols[0], a_cols[1]))
            y_tiles.append(y_q + y_off[:, qcols] * e_q)

        y = jnp.concatenate(y_tiles, axis=1) + xg * dsk_ref[:, gcols]
        y = y * zs_ref[:, gcols]
        y = y * lax.rsqrt(jnp.mean(y * y, axis=-1, keepdims=True) + EPS)
        y_ref[:, gcols] = (y * ng_ref[:, gcols]).astype(BF16)


def _ssd_core(zx, dtt, dt_bias, a_log, d_skip, norm_g, *, batch, seq):
    n_tok = zx.shape[0]
    inner = norm_g.shape[1]
    heads = dtt.shape[0]
    nc = seq // SSM_CHUNK
    bc_w = SSM_GROUPS * SSM_STATE
    tok = lambda b, c: b * nc + c
    col = lambda v: v.reshape(-1, 1)
    d_lanes = jnp.repeat(d_skip, SSM_HEAD_DIM).reshape(1, inner)
    blocks = [((SSM_CHUNK, inner), F32)] * 2 + [((SSM_CHUNK, bc_w), F32)] * 2 + \
             [((heads, SSM_CHUNK), F32), ((SSM_CHUNK, inner), BF16)]
    scratch = [((inner, SSM_STATE), F32)]
    return pl.pallas_call(
        _ssd_core_kernel,
        grid=(batch, nc),
        in_specs=[
            pl.BlockSpec((SSM_CHUNK, inner), lambda b, c: (tok(b, c), 0)),
            pl.BlockSpec((SSM_CHUNK, inner), lambda b, c: (tok(b, c), 1)),
            pl.BlockSpec((SSM_CHUNK, bc_w), lambda b, c: (tok(b, c), 2 * inner // bc_w)),
            pl.BlockSpec((SSM_CHUNK, bc_w), lambda b, c: (tok(b, c), 2 * inner // bc_w + 1)),
            pl.BlockSpec((heads, SSM_CHUNK), lambda b, c: (0, tok(b, c))),
            pl.BlockSpec((heads, 1), lambda b, c: (0, 0)),
            pl.BlockSpec((heads, 1), lambda b, c: (0, 0)),
            pl.BlockSpec((1, inner), lambda b, c: (0, 0)),
            pl.BlockSpec((1, inner), lambda b, c: (0, 0)),
        ],
        out_specs=pl.BlockSpec((SSM_CHUNK, inner), lambda b, c: (tok(b, c), 0)),
        out_shape=jax.ShapeDtypeStruct((n_tok, inner), BF16),
        scratch_shapes=[pltpu.VMEM(s, t) for s, t in scratch],
        compiler_params=_params(("parallel", "arbitrary"), blocks, scratch, temps=8 << 20),
        name="ssd_core",
    )(zx, zx, zx, zx, dtt, col(dt_bias), col(a_log), d_lanes, norm_g)


def _proj_res_kernel(y_ref, w_ref, x_ref, o_ref, wb_ref):
    @pl.when(pl.program_id(1) == 0)
    def _():
        wb_ref[...] = w_ref[...].astype(BF16)

    o_ref[...] = x_ref[...] + jnp.dot(y_ref[...], wb_ref[...], preferred_element_type=F32)


def _proj_res(y, w, x, *, layer, tm, tn):
    n_tok, k = y.shape
    d = w.shape[2]
    blocks = [((tm, k), BF16), ((k, tn), F32), ((tm, tn), F32), ((tm, tn), F32)]
    scratch = [((k, tn), BF16)]
    return pl.pallas_call(
        _proj_res_kernel,
        grid=(d // tn, n_tok // tm),
        in_specs=[
            pl.BlockSpec((tm, k), lambda j, i: (i, 0)),
            pl.BlockSpec((None, k, tn), lambda j, i: (layer, 0, j)),
            pl.BlockSpec((tm, tn), lambda j, i: (i, j)),
        ],
        out_specs=pl.BlockSpec((tm, tn), lambda j, i: (i, j)),
        out_shape=jax.ShapeDtypeStruct((n_tok, d), F32),
        scratch_shapes=[pltpu.VMEM(s, t) for s, t in scratch],
        compiler_params=_params(("parallel", "arbitrary"), blocks, scratch, temps=tm * tn * 4),
        name="proj_res",
    )(y, w, x)


def kernel(x, ln_ffn_pre, ffn_pre_w_in, ffn_pre_w_out, ln_mix, ln_ffn_post, ffn_post_w_in,
           ffn_post_w_out, gm_w_in, gm_v_norm, gm_w_s, gm_b_s, gm_w_out, ssm_w_in, ssm_conv_w,
           ssm_conv_b, ssm_dt_bias, ssm_a_log, ssm_d, ssm_norm, ssm_w_out, ln_final):
    batch, seq, d = x.shape
    depth = ln_mix.shape[0]
    n_tok = batch * seq
    inner = ssm_norm.shape[1]
    heads = ssm_dt_bias.shape[1]
    zx_cols = ssm_conv_w.shape[1] + inner
    assert seq % SSM_CHUNK == 0 and seq % GM_CHUNK == 0
    assert heads == SSM_GROUPS * SSM_HEADS_PER_GROUP and inner == heads * SSM_HEAD_DIM

    row = lambda v: v.reshape(1, -1)
    ssm_w_in_t = jnp.swapaxes(ssm_w_in, 1, 2)
    xf = x.reshape(n_tok, d)
    for i in range(depth):
        xf = _ffn(xf, row(ln_ffn_pre[i]), ffn_pre_w_in, ffn_pre_w_out, row(ln_final),
                  layer=i, tm=2048, tf=256, final_norm=False)
        m = i // 2
        if i % 2 == 0:
            u, v, ssq = _gm_in(xf, row(ln_mix[i]), gm_w_in, layer=m, tm=1024, tn=1024)
            xf = _gm_out(xf, u, v, ssq.reshape(1, n_tok), row(gm_v_norm[m]), gm_w_s[m], gm_b_s[m],
                         gm_w_out, layer=m, tm=2048, tc=256)
        else:
            zx, dtt = _ssd_in(xf, row(ln_mix[i]), ssm_w_in_t, ssm_conv_w[m].T,
                              row(ssm_conv_b[m]), layer=m, n_out=zx_cols, tm=1024, tn=1024, seq=seq)
            y = _ssd_core(zx, dtt, ssm_dt_bias[m], ssm_a_log[m], ssm_d[m], row(ssm_norm[m]),
                          batch=batch, seq=seq)
            xf = _proj_res(y, ssm_w_out, xf, layer=m, tm=512, tn=1024)
        xf = _ffn(xf, row(ln_ffn_post[i]), ffn_post_w_in, ffn_post_w_out, row(ln_final),
                  layer=i, tm=2048, tf=256, final_norm=(i == depth - 1))
    return xf.reshape(batch, seq, d)
```

```python
import functools
import math

import jax
import jax.numpy as jnp
from jax import lax
from jax.experimental import pallas as pl
from jax.experimental.pallas import tpu as pltpu

F32 = jnp.float32
BF16 = jnp.bfloat16
EPS = 1e-6

V7X_VMEM_BYTES = 64 * 1024 * 1024
V7X_LANES = 128
V7X_MXU_COLS = 256
MXU_ROW_BLOCK = 512

GM_CHUNK = 128
GM_GROUPS = 16
SSM_HEAD_DIM = 64
SSM_GROUPS = 8
SSM_HEADS_PER_GROUP = 8
SSM_STATE = 128
SSM_CONV = 4
SSM_CHUNK = 128
SSM_GROUP_WIDTH = SSM_HEADS_PER_GROUP * SSM_HEAD_DIM
HEADS_PER_LANE_TILE = V7X_LANES // SSM_HEAD_DIM
CONV_HALO = 8


def _nbytes(shape, dtype):
    return math.prod(shape) * jnp.dtype(dtype).itemsize


def _params(semantics, blocks, scratch=(), temps=0, flags=None):
    need = 2 * sum(_nbytes(s, d) for s, d in blocks)
    need += sum(_nbytes(s, d) for s, d in scratch) + temps
    limit = min(V7X_VMEM_BYTES - (4 << 20), max(32 << 20, need + (8 << 20)))
    return pltpu.CompilerParams(dimension_semantics=semantics, vmem_limit_bytes=limit,
                                flags=flags)


_NT = (((1,), (1,)), ((), ()))


def _sub_tiles(n_cols, n_rows):
    cols = [slice(c, c + V7X_MXU_COLS) for c in range(0, n_cols, V7X_MXU_COLS)]
    rows = [slice(r, r + MXU_ROW_BLOCK) for r in range(0, n_rows, MXU_ROW_BLOCK)]
    return cols, rows


def _rmsnorm_rows(x, gain):
    ms = jnp.mean(x * x, axis=-1, keepdims=True)
    return x * lax.rsqrt(ms + EPS) * gain


ACC_ROW_BLOCK = 1024


class _ResidualSlots:
    def __init__(self, x_hbm, o_hbm, acc_ref, in_sem, out_sem):
        self.i = pl.program_id(0)
        self.j = pl.program_id(1)
        self.n_i = pl.num_programs(0)
        self.n_j = pl.num_programs(1)
        self.tm = acc_ref.shape[1]
        self.slot = self.i % 2
        self.other = 1 - self.slot
        self.acc = acc_ref.at[self.slot]
        self._refs = (x_hbm, o_hbm, acc_ref, in_sem, out_sem)

    def _x_copy(self, tile, s):
        x_hbm, _, acc_ref, in_sem, _ = self._refs
        return pltpu.make_async_copy(x_hbm.at[pl.ds(tile * self.tm, self.tm), :],
                                     acc_ref.at[s], in_sem.at[s])

    def _out_copy(self, tile, s):
        _, o_hbm, acc_ref, _, out_sem = self._refs
        return pltpu.make_async_copy(acc_ref.at[s],
                                     o_hbm.at[pl.ds(tile * self.tm, self.tm), :], out_sem.at[s])

    def begin_step(self, on_tile_loaded=None):
        i, j = self.i, self.j

        @pl.when((i == 0) & (j == 0))
        def _():
            self._x_copy(0, 0).start()

        @pl.when(j == 0)
        def _():
            self._x_copy(i, self.slot).wait()
            if on_tile_loaded is not None:
                on_tile_loaded()

        @pl.when((j == 1) & (i + 1 < self.n_i))
        def _():
            @pl.when(i >= 1)
            def _():
                self._out_copy(i - 1, self.other).wait()

            self._x_copy(i + 1, self.other).start()

    def end_step(self, before_write_back=None):
        i = self.i

        @pl.when(self.j == self.n_j - 1)
        def _():
            if before_write_back is not None:
                before_write_back()
            self._out_copy(i, self.slot).start()

            @pl.when(i == self.n_i - 1)
            def _():
                @pl.when(self.n_i >= 2)
                def _():
                    self._out_copy(i - 1, self.other).wait()

                self._out_copy(i, self.slot).wait()


def _ffn_kernel(x_hbm, g_ref, fg_ref, wg_ref, wu_ref, wo_ref, o_hbm, acc_ref, h_ref,
                in_sem, out_sem, *, final_norm):
    slots = _ResidualSlots(x_hbm, o_hbm, acc_ref, in_sem, out_sem)
    acc = slots.acc
    row_blocks = [slice(r, r + ACC_ROW_BLOCK) for r in range(0, slots.tm, ACC_ROW_BLOCK)]

    def norm_input():
        for rows in row_blocks:
            h_ref[rows, :] = _rmsnorm_rows(acc[rows, :], g_ref[...]).astype(BF16)

    slots.begin_step(on_tile_loaded=norm_input)

    wg = wg_ref[...].astype(BF16)
    wu = wu_ref[...].astype(BF16)
    wo = wo_ref[...].astype(BF16)
    for rows in row_blocks:
        h = h_ref[rows, :]
        gate = jnp.dot(h, wg, preferred_element_type=F32)
        up = jnp.dot(h, wu, preferred_element_type=F32)
        act = (gate * jax.nn.sigmoid(gate) * 0.5 * up).astype(BF16)
        acc[rows, :] += jnp.dot(act, wo, preferred_element_type=F32)

    def norm_output():
        for rows in row_blocks:
            acc[rows, :] = _rmsnorm_rows(acc[rows, :], fg_ref[...])

    slots.end_step(before_write_back=norm_output if final_norm else None)


def _ffn(x, gain, w_in, w_out, final_gain, *, layer, tm, tf, final_norm):
    n_tok, d = x.shape
    d_ff = w_out.shape[1]
    nf = d_ff // tf
    assert nf >= 2 and tm % ACC_ROW_BLOCK == 0
    blocks = [((1, d), F32), ((1, d), F32), ((d, tf), F32), ((d, tf), F32), ((tf, d), F32)]
    scratch = [((2, tm, d), F32), ((tm, d), BF16)]
    return pl.pallas_call(
        functools.partial(_ffn_kernel, final_norm=final_norm),
        grid=(n_tok // tm, nf),
        in_specs=[
            pl.BlockSpec(memory_space=pl.ANY),
            pl.BlockSpec((1, d), lambda i, j: (0, 0)),
            pl.BlockSpec((1, d), lambda i, j: (0, 0)),
            pl.BlockSpec((None, d, tf), lambda i, j: (layer, 0, j)),
            pl.BlockSpec((None, d, tf), lambda i, j: (layer, 0, j + nf)),
            pl.BlockSpec((None, tf, d), lambda i, j: (layer, j, 0)),
        ],
        out_specs=pl.BlockSpec(memory_space=pl.ANY),
        out_shape=jax.ShapeDtypeStruct((n_tok, d), F32),
        scratch_shapes=[pltpu.VMEM(s, t) for s, t in scratch]
        + [pltpu.SemaphoreType.DMA((2,)), pltpu.SemaphoreType.DMA((2,))],
        compiler_params=_params(("arbitrary", "arbitrary"), blocks, scratch,
                                temps=4 * ACC_ROW_BLOCK * tf * 4),
        name="ffn",
    )(x, gain, final_gain, w_in, w_in, w_out)


def _gm_in_kernel(x_ref, g_ref, w_ref, u_ref, v_ref, ssq_ref, h_ref, *, nu):
    j = pl.program_id(1)

    @pl.when(j == 0)
    def _():
        h_ref[...] = _rmsnorm_rows(x_ref[...], g_ref[...]).astype(BF16)
        ssq_ref[...] = jnp.zeros_like(ssq_ref)

    def gelu_tile(rows, w):
        z = jnp.dot(h_ref[rows, :], w, preferred_element_type=F32)
        return 0.5 * z * (1.0 + lax.erf(z * math.sqrt(0.5)))

    col_tiles, row_tiles = _sub_tiles(w_ref.shape[1], h_ref.shape[0])

    @pl.when(j < nu)
    def _():
        for cols in col_tiles:
            w = w_ref[:, cols].astype(BF16)
            for rows in row_tiles:
                u_ref[rows, cols] = gelu_tile(rows, w)

    @pl.when(j >= nu)
    def _():
        ssq = [jnp.zeros((rows.stop - rows.start, 1), F32) for rows in row_tiles]
        for cols in col_tiles:
            w = w_ref[:, cols].astype(BF16)
            for r, rows in enumerate(row_tiles):
                v = gelu_tile(rows, w)
                v_ref[rows, cols] = v.astype(BF16)
                ssq[r] += jnp.sum(v * v, axis=-1, keepdims=True)
        for r, rows in enumerate(row_tiles):
            ssq_ref[rows, :] += ssq[r]


def _gm_in(x, gain, w_in, *, layer, tm, tn):
    n_tok, d = x.shape
    inner = w_in.shape[2] // 2
    nu = inner // tn
    blocks = [((tm, d), F32), ((1, d), F32), ((d, tn), F32), ((tm, tn), F32), ((tm, tn), BF16),
              ((tm, V7X_LANES), F32)]
    scratch = [((tm, d), BF16)]
    return pl.pallas_call(
        functools.partial(_gm_in_kernel, nu=nu),
        grid=(n_tok // tm, 2 * nu),
        in_specs=[
            pl.BlockSpec((tm, d), lambda i, j: (i, 0)),
            pl.BlockSpec((1, d), lambda i, j: (0, 0)),
            pl.BlockSpec((None, d, tn), lambda i, j: (layer, 0, j)),
        ],
        out_specs=[
            pl.BlockSpec((tm, tn), lambda i, j: (i, jnp.minimum(j, nu - 1))),
            pl.BlockSpec((tm, tn), lambda i, j: (i, jnp.maximum(j - nu, 0))),
            pl.BlockSpec((tm, 1), lambda i, j: (i, 0)),
        ],
        out_shape=[jax.ShapeDtypeStruct((n_tok, inner), F32),
                   jax.ShapeDtypeStruct((n_tok, inner), BF16),
                   jax.ShapeDtypeStruct((n_tok, 1), F32)],
        scratch_shapes=[pltpu.VMEM(s, t) for s, t in scratch],
        compiler_params=_params(("parallel", "arbitrary"), blocks, scratch,
                                temps=4 * tm * tn * 4 + d * tn * 2),
        name="gm_in",
    )(x, gain, w_in)


def _gm_out_kernel(x_hbm, u_ref, v_ref, ssq_ref, vg_ref, ws_ref, bs_ref, wo_ref, o_hbm,
                   acc_ref, gated_ref, in_sem, out_sem, *, inner):
    slots = _ResidualSlots(x_hbm, o_hbm, acc_ref, in_sem, out_sem)
    slots.begin_step()
    acc = slots.acc

    groups = ws_ref.shape[0]
    gw = v_ref.shape[1] // groups
    tm = v_ref.shape[0]
    t_idx = lax.broadcasted_iota(jnp.int32, (GM_CHUNK, GM_CHUNK), 0)
    s_idx = lax.broadcasted_iota(jnp.int32, (GM_CHUNK, GM_CHUNK), 1)
    causal = s_idx <= t_idx
    scale = lax.rsqrt(ssq_ref[...] / inner + EPS)
    for g in range(groups):
        w = jnp.where(causal, ws_ref[g], 0.0)
        b = bs_ref[g]
        cols = slice(g * gw, (g + 1) * gw)
        gain = vg_ref[:, cols]
        for c in range(tm // GM_CHUNK):
            rows = slice(c * GM_CHUNK, (c + 1) * GM_CHUNK)
            w_c = (w * scale[:, rows]).astype(BF16)
            mixed = jnp.dot(w_c, v_ref[rows, cols], preferred_element_type=F32) * gain + b
            gated_ref[rows, cols] = (u_ref[rows, cols] * mixed).astype(BF16)
    wo = wo_ref[...].astype(BF16)
    for r in range(0, tm, ACC_ROW_BLOCK):
        rows = slice(r, r + ACC_ROW_BLOCK)
        acc[rows, :] += jnp.dot(gated_ref[rows, :], wo, preferred_element_type=F32)
    slots.end_step()


def _gm_out(x, u, v, ssq_row, v_gain, w_s, b_s, w_out, *, layer, tm, tc):
    n_tok, d = x.shape
    inner = u.shape[1]
    gw = inner // GM_GROUPS
    gpt = tc // gw
    assert inner // tc >= 2 and tm % ACC_ROW_BLOCK == 0
    blocks = [((tm, tc), F32), ((tm, tc), BF16), ((8, tm), F32), ((8, tc), F32),
              ((gpt, GM_CHUNK, GM_CHUNK), F32), ((gpt, GM_CHUNK, V7X_LANES), F32),
              ((tc, d), F32)]
    scratch = [((2, tm, d), F32), ((tm, tc), BF16)]
    return pl.pallas_call(
        functools.partial(_gm_out_kernel, inner=inner),
        grid=(n_tok // tm, inner // tc),
        in_specs=[
            pl.BlockSpec(memory_space=pl.ANY),
            pl.BlockSpec((tm, tc), lambda i, j: (i, j)),
            pl.BlockSpec((tm, tc), lambda i, j: (i, j)),
            pl.BlockSpec((1, tm), lambda i, j: (0, i)),
            pl.BlockSpec((1, tc), lambda i, j: (0, j)),
            pl.BlockSpec((gpt, GM_CHUNK, GM_CHUNK), lambda i, j: (j, 0, 0)),
            pl.BlockSpec((gpt, GM_CHUNK, 1), lambda i, j: (j, 0, 0)),
            pl.BlockSpec((None, tc, d), lambda i, j: (layer, j, 0)),
        ],
        out_specs=pl.BlockSpec(memory_space=pl.ANY),
        out_shape=jax.ShapeDtypeStruct((n_tok, d), F32),
        scratch_shapes=[pltpu.VMEM(s, t) for s, t in scratch]
        + [pltpu.SemaphoreType.DMA((2,)), pltpu.SemaphoreType.DMA((2,))],
        compiler_params=_params(("arbitrary", "arbitrary"), blocks, scratch,
                                temps=2 * ACC_ROW_BLOCK * tc * 4 + tc * d * 2),
        name="gm_out",
    )(x, u, v, ssq_row, v_gain, w_s, b_s[..., None], w_out)


def _ssd_in_kernel(x_ref, g_ref, w_ref, wdt_ref, cw_ref, cb_ref, zx_ref, dt_ref,
                   h_ref, pad_ref, carry_ref, *, tiles_per_seq, nz):
    i = pl.program_id(0)
    j = pl.program_id(1)
    tm = zx_ref.shape[0]
    seq_start = (i % tiles_per_seq) == 0

    @pl.when(j == 0)
    def _():
        h_ref[...] = _rmsnorm_rows(x_ref[...], g_ref[...]).astype(BF16)
        dt_ref[...] = lax.dot_general(wdt_ref[...].astype(BF16), h_ref[...], _NT,
                                      preferred_element_type=F32)

    col_tiles, row_tiles = _sub_tiles(w_ref.shape[0], tm)

    def project(rows, w):
        return lax.dot_general(h_ref[rows, :], w, _NT, preferred_element_type=F32)

    @pl.when(j < nz)
    def _():
        for cols in col_tiles:
            w = w_ref[cols, :].astype(BF16)
            for rows in row_tiles:
                acc = project(rows, w)
                zx_ref[rows, cols] = acc * jax.nn.sigmoid(acc)

    @pl.when(j >= nz)
    def _():
        t = j - nz

        @pl.when(seq_start)
        def _():
            pad_ref[0:CONV_HALO, :] = jnp.zeros((CONV_HALO, pad_ref.shape[1]), F32)

        @pl.when(jnp.logical_not(seq_start))
        def _():
            pad_ref[0:CONV_HALO, :] = carry_ref[t]

        for cols in col_tiles:
            w = w_ref[cols, :].astype(BF16)
            for rows in row_tiles:
                acc = project(rows, w)
                n = rows.stop - rows.start
                pad_ref[CONV_HALO + rows.start:CONV_HALO + rows.stop, cols] = acc
                out = cb_ref[:, cols] + cw_ref[SSM_CONV - 1:SSM_CONV, cols] * acc
                for k in range(SSM_CONV - 1):
                    off = rows.start + CONV_HALO - (SSM_CONV - 1) + k
                    out += cw_ref[k:k + 1, cols] * pad_ref[off:off + n, cols]
                zx_ref[rows, cols] = out * jax.nn.sigmoid(out)
            carry_ref[t, :, cols] = pad_ref[tm:tm + CONV_HALO, cols]


def _ssd_in(x, gain, w_in_t, conv_w_t, conv_b, *, layer, n_out, tm, tn, seq):
    n_tok, d = x.shape
    heads = w_in_t.shape[1] - n_out
    nj = n_out // tn
    nz = nj - conv_w_t.shape[1] // tn
    blocks = [((tm, d), F32), ((1, d), F32), ((tn, d), F32), ((heads, d), F32),
              ((8, tn), F32), ((8, tn), F32), ((tm, tn), F32), ((heads, tm), F32)]
    scratch = [((tm, d), BF16), ((tm + CONV_HALO, tn), F32), ((nj - nz, CONV_HALO, tn), F32)]
    return pl.pallas_call(
        functools.partial(_ssd_in_kernel, tiles_per_seq=seq // tm, nz=nz),
        grid=(n_tok // tm, nj),
        in_specs=[
            pl.BlockSpec((tm, d), lambda i, j: (i, 0)),
            pl.BlockSpec((1, d), lambda i, j: (0, 0)),
            pl.BlockSpec((None, tn, d), lambda i, j: (layer, j, 0)),
            pl.BlockSpec((None, heads, d), lambda i, j: (layer, n_out // heads, 0)),
            pl.BlockSpec((SSM_CONV, tn), lambda i, j: (0, jnp.maximum(j - nz, 0))),
            pl.BlockSpec((1, tn), lambda i, j: (0, jnp.maximum(j - nz, 0))),
        ],
        out_specs=[
            pl.BlockSpec((tm, tn), lambda i, j: (i, j)),
            pl.BlockSpec((heads, tm), lambda i, j: (0, i)),
        ],
        out_shape=[jax.ShapeDtypeStruct((n_tok, n_out), F32),
                   jax.ShapeDtypeStruct((heads, n_tok), F32)],
        scratch_shapes=[pltpu.VMEM(s, t) for s, t in scratch],
        compiler_params=_params(("arbitrary", "arbitrary"), blocks, scratch,
                                temps=3 * tm * tn * 4 + d * tn * 2),
        name="ssd_in",
    )(x, gain, w_in_t, w_in_t, conv_w_t, conv_b)


def _ssd_core_kernel(zs_ref, xs_ref, b_ref, c_ref, dtt_ref, dtb_ref, al_ref, dsk_ref, ng_ref,
                     y_ref, state_ref):
    L = SSM_CHUNK
    P = SSM_HEAD_DIM
    N = SSM_STATE
    hpg = SSM_HEADS_PER_GROUP
    gw = SSM_GROUP_WIDTH
    heads = dtt_ref.shape[0]

    @pl.when(pl.program_id(1) == 0)
    def _():
        state_ref[...] = jnp.zeros_like(state_ref)

    dtt = jax.nn.softplus(dtt_ref[...] + dtb_ref[...])
    at = -jnp.exp(al_ref[...])
    r_idx = lax.broadcasted_iota(jnp.int32, (L, L), 0)
    c_idx = lax.broadcasted_iota(jnp.int32, (L, L), 1)
    causal = c_idx <= r_idx
    upper = (r_idx <= c_idx).astype(F32)
    a_cum_t = jnp.dot(dtt * at, upper, preferred_element_type=F32,
                      precision=lax.Precision.HIGHEST)
    last_t = a_cum_t[:, L - 1:L]
    w_t = dtt * jnp.exp(last_t - a_cum_t)
    dk_b = jnp.exp(jnp.broadcast_to(last_t, (heads, N)))
    stack_t = jnp.concatenate(
        [a_cum_t, jnp.zeros((V7X_LANES - heads, L), F32)], axis=0).T
    low_half = lax.broadcasted_iota(jnp.int32, (L, V7X_LANES), 1) < P

    for g in range(SSM_GROUPS):
        gcols = slice(g * gw, (g + 1) * gw)
        bm = b_ref[:, g * N:(g + 1) * N].astype(BF16)
        cm = c_ref[:, g * N:(g + 1) * N].astype(BF16)
        xg = xs_ref[:, gcols]
        st = state_ref[gcols, :]
        hs = range(g * hpg, (g + 1) * hpg)

        cb = lax.dot_general(cm, bm, (((1,), (1,)), ((), ())),
                             preferred_element_type=F32)
        y_off = lax.dot_general(cm, st.astype(BF16), (((1,), (1,)), ((), ())),
                                preferred_element_type=F32)

        w_rows = jnp.concatenate(
            [jnp.broadcast_to(w_t[h:h + 1, :], (P, L)) for h in hs], axis=0)
        dk_rows = jnp.concatenate(
            [jnp.broadcast_to(dk_b[h:h + 1, :], (P, N)) for h in hs], axis=0)
        xw_t = (xg.T * w_rows).astype(BF16)
        state_ref[gcols, :] = st * dk_rows + jnp.dot(xw_t, bm, preferred_element_type=F32)

        y_tiles = []
        for q in range(hpg // HEADS_PER_LANE_TILE):
            qcols = slice(q * V7X_LANES, (q + 1) * V7X_LANES)
            x_q = xg[:, qcols]
            a_cols = []
            scores = []
            for hh in range(HEADS_PER_LANE_TILE):
                h = g * hpg + q * HEADS_PER_LANE_TILE + hh
                a_col = jnp.broadcast_to(stack_t[:, h:h + 1], (L, L))
                seg = a_col - a_cum_t[h:h + 1, :]
                decay = jnp.exp(jnp.where(causal, seg, -jnp.inf))
                scores.append((cb * decay * dtt[h:h + 1, :]).astype(BF16))
                a_cols.append(a_col)
            x_lo = jnp.where(low_half, x_q, 0.0).astype(BF16)
            x_hi = jnp.where(low_half, 0.0, x_q).astype(BF16)
            y_q = jnp.dot(jnp.concatenate(scores, axis=1),
                          jnp.concatenate([x_lo, x_hi], axis=0),
                          preferred_element_type=F32)
            e_q = jnp.exp(jnp.where(low_half, a_cols[0], a_cols[1]))
            y_tiles.append(y_q + y_off[:, qcols] * e_q)

        y = jnp.concatenate(y_tiles, axis=1) + xg * dsk_ref[:, gcols]
        y = y * zs_ref[:, gcols]
        y = y * lax.rsqrt(jnp.mean(y * y, axis=-1, keepdims=True) + EPS)
        y_ref[:, gcols] = (y * ng_ref[:, gcols]).astype(BF16)


def _ssd_core(zx, dtt, dt_bias, a_log, d_skip, norm_g, *, batch, seq):
    n_tok = zx.shape[0]
    inner = norm_g.shape[1]
    heads = dtt.shape[0]
    nc = seq // SSM_CHUNK
    bc_w = SSM_GROUPS * SSM_STATE
    tok = lambda b, c: b * nc + c
    col = lambda v: v.reshape(-1, 1)
    d_lanes = jnp.repeat(d_skip, SSM_HEAD_DIM).reshape(1, inner)
    blocks = [((SSM_CHUNK, inner), F32)] * 2 + [((SSM_CHUNK, bc_w), F32)] * 2 + \
             [((heads, SSM_CHUNK), F32), ((SSM_CHUNK, inner), BF16)]
    scratch = [((inner, SSM_STATE), F32)]
    return pl.pallas_call(
        _ssd_core_kernel,
        grid=(batch, nc),
        in_specs=[
            pl.BlockSpec((SSM_CHUNK, inner), lambda b, c: (tok(b, c), 0)),
            pl.BlockSpec((SSM_CHUNK, inner), lambda b, c: (tok(b, c), 1)),
            pl.BlockSpec((SSM_CHUNK, bc_w), lambda b, c: (tok(b, c), 2 * inner // bc_w)),
            pl.BlockSpec((SSM_CHUNK, bc_w), lambda b, c: (tok(b, c), 2 * inner // bc_w + 1)),
            pl.BlockSpec((heads, SSM_CHUNK), lambda b, c: (0, tok(b, c))),
            pl.BlockSpec((heads, 1), lambda b, c: (0, 0)),
            pl.BlockSpec((heads, 1), lambda b, c: (0, 0)),
            pl.BlockSpec((1, inner), lambda b, c: (0, 0)),
            pl.BlockSpec((1, inner), lambda b, c: (0, 0)),
        ],
        out_specs=pl.BlockSpec((SSM_CHUNK, inner), lambda b, c: (tok(b, c), 0)),
        out_shape=jax.ShapeDtypeStruct((n_tok, inner), BF16),
        scratch_shapes=[pltpu.VMEM(s, t) for s, t in scratch],
        compiler_params=_params(("parallel", "arbitrary"), blocks, scratch, temps=8 << 20),
        name="ssd_core",
    )(zx, zx, zx, zx, dtt, col(dt_bias), col(a_log), d_lanes, norm_g)


def _proj_res_kernel(y_ref, w_ref, x_ref, o_ref, wb_ref):
    @pl.when(pl.program_id(1) == 0)
    def _():
        wb_ref[...] = w_ref[...].astype(BF16)

    o_ref[...] = x_ref[...] + jnp.dot(y_ref[...], wb_ref[...], preferred_element_type=F32)


def _proj_res(y, w, x, *, layer, tm, tn):
    n_tok, k = y.shape
    d = w.shape[2]
    blocks = [((tm, k), BF16), ((k, tn), F32), ((tm, tn), F32), ((tm, tn), F32)]
    scratch = [((k, tn), BF16)]
    return pl.pallas_call(
        _proj_res_kernel,
        grid=(d // tn, n_tok // tm),
        in_specs=[
            pl.BlockSpec((tm, k), lambda j, i: (i, 0)),
            pl.BlockSpec((None, k, tn), lambda j, i: (layer, 0, j)),
            pl.BlockSpec((tm, tn), lambda j, i: (i, j)),
        ],
        out_specs=pl.BlockSpec((tm, tn), lambda j, i: (i, j)),
        out_shape=jax.ShapeDtypeStruct((n_tok, d), F32),
        scratch_shapes=[pltpu.VMEM(s, t) for s, t in scratch],
        compiler_params=_params(("parallel", "arbitrary"), blocks, scratch, temps=tm * tn * 4),
        name="proj_res",
    )(y, w, x)


def kernel(x, ln_ffn_pre, ffn_pre_w_in, ffn_pre_w_out, ln_mix, ln_ffn_post, ffn_post_w_in,
           ffn_post_w_out, gm_w_in, gm_v_norm, gm_w_s, gm_b_s, gm_w_out, ssm_w_in, ssm_conv_w,
           ssm_conv_b, ssm_dt_bias, ssm_a_log, ssm_d, ssm_norm, ssm_w_out, ln_final):
    batch, seq, d = x.shape
    depth = ln_mix.shape[0]
    n_tok = batch * seq
    inner = ssm_norm.shape[1]
    heads = ssm_dt_bias.shape[1]
    zx_cols = ssm_conv_w.shape[1] + inner
    assert seq % SSM_CHUNK == 0 and seq % GM_CHUNK == 0
    assert heads == SSM_GROUPS * SSM_HEADS_PER_GROUP and inner == heads * SSM_HEAD_DIM

    row = lambda v: v.reshape(1, -1)
    ssm_w_in_t = jnp.swapaxes(ssm_w_in, 1, 2)
    xf = x.reshape(n_tok, d)
    for i in range(depth):
        xf = _ffn(xf, row(ln_ffn_pre[i]), ffn_pre_w_in, ffn_pre_w_out, row(ln_final),
                  layer=i, tm=2048, tf=256, final_norm=False)
        m = i // 2
        if i % 2 == 0:
            u, v, ssq = _gm_in(xf, row(ln_mix[i]), gm_w_in, layer=m, tm=1024, tn=1024)
            xf = _gm_out(xf, u, v, ssq.reshape(1, n_tok), row(gm_v_norm[m]), gm_w_s[m], gm_b_s[m],
                         gm_w_out, layer=m, tm=2048, tc=512)
        else:
            zx, dtt = _ssd_in(xf, row(ln_mix[i]), ssm_w_in_t, ssm_conv_w[m].T,
                              row(ssm_conv_b[m]), layer=m, n_out=zx_cols, tm=1024, tn=1024, seq=seq)
            y = _ssd_core(zx, dtt, ssm_dt_bias[m], ssm_a_log[m], ssm_d[m], row(ssm_norm[m]),
                          batch=batch, seq=seq)
            xf = _proj_res(y, ssm_w_out, xf, layer=m, tm=512, tn=1024)
        xf = _ffn(xf, row(ln_ffn_post[i]), ffn_post_w_in, ffn_post_w_out, row(ln_final),
                  layer=i, tm=2048, tf=256, final_norm=(i == depth - 1))
    return xf.reshape(batch, seq, d)
```
